```python
import math
import jax, jax.numpy as jnp
from jax import lax
import numpy as np

D_MODEL = 1024
BATCH = 2
SEQ = 8192
DEPTH = 4
DEC_BATCH = 128
DEC_SEQ = 1
PAST_LEN = 8192
PAGE_SIZE = 128

HEAD_DIM = 64
N_HEADS = D_MODEL // HEAD_DIM
A_HEADS = N_HEADS // 2
A_KV_HEADS = 2
B_HEADS = N_HEADS - A_HEADS
WIN_A = 128
DIL_PAIRS = ((128, 1), (512, 4), (2048, 16))
DIL_MAX_WIN = 2048
NSA_HEADS = N_HEADS
NSA_KV_HEADS = 2
CMP_STRIDE = 16
CMP_LEN = 2 * CMP_STRIDE
CMP_HID = 2 * HEAD_DIM
SLC_BLOCK = 64
N_SELECT = 16
WIN_C = 512
D_FF = 4 * D_MODEL
RP_BUCKETS = 32
RP_MAX_DIST = 2048
Q_BLOCK = 128
N_EVEN = (DEPTH + 1) // 2
N_ODD = DEPTH // 2
EPS = 1e-6
NEG = -1e30
FORCE = 1e6
F32 = jnp.float32
EVEN_SPLITS = (A_HEADS * HEAD_DIM, A_KV_HEADS * HEAD_DIM, A_KV_HEADS * HEAD_DIM,
               B_HEADS * HEAD_DIM, B_HEADS * HEAD_DIM, B_HEADS * HEAD_DIM)
NSA_SPLITS = (NSA_HEADS * HEAD_DIM, 3 * 2 * NSA_KV_HEADS * HEAD_DIM, NSA_HEADS * 3)

kernel_name = 'hybrid_swa_dilated_nsa_decoder_step'


def _split(z, sizes):
    out, o = [], 0
    for s in sizes:
        out.append(z[..., o:o + s])
        o += s
    return out


def rmsnorm(x, g):
    xf = x.astype(F32)
    y = xf * lax.rsqrt(jnp.mean(xf * xf, axis=-1, keepdims=True) + EPS)
    return (y * g.astype(F32)).astype(x.dtype)


def t5_bucket(dist):
    n = dist.astype(jnp.int32)
    exact = RP_BUCKETS // 2
    val = jnp.log(jnp.maximum(n, 1).astype(F32) / exact) / math.log(RP_MAX_DIST / exact)
    large = jnp.minimum(exact + (val * (RP_BUCKETS - exact)).astype(jnp.int32), RP_BUCKETS - 1)
    return jnp.where(n < exact, n, large)


def rp_bias(bias_tab, dist):
    return bias_tab[t5_bucket(dist)].astype(F32)


def merge_stats(stats, sink=None):
    big_m = stats[0][0]
    for m, _, _ in stats[1:]:
        big_m = jnp.maximum(big_m, m)
    if sink is not None:
        sink = sink.astype(F32)
        big_m = jnp.maximum(big_m, sink)
    l_tot, acc_tot = 0.0, 0.0
    for m, l, acc in stats:
        w = jnp.exp(m - big_m)
        l_tot = l_tot + l * w
        acc_tot = acc_tot + acc * w[..., None]
    if sink is not None:
        l_tot = l_tot + jnp.exp(sink - big_m)
    return acc_tot / l_tot[..., None]


def banded_stats(q, k, v, window, bias_tab, dist_scale):
    n, L, hq, hd = q.shape
    hkv = k.shape[2]
    g = hq // hkv
    nq = -(-L // Q_BLOCK)
    lp = nq * Q_BLOCK
    nb = -(-window // Q_BLOCK)
    kw = (nb + 1) * Q_BLOCK
    pad_kv = ((0, 0), (nb * Q_BLOCK, lp - L), (0, 0), (0, 0))
    kp = jnp.pad(k, pad_kv).reshape(n, nb + nq, Q_BLOCK, hkv, hd)
    vp = jnp.pad(v, pad_kv).reshape(n, nb + nq, Q_BLOCK, hkv, hd)
    kb = jnp.concatenate([kp[:, j:j + nq] for j in range(nb + 1)], axis=2)
    vb = jnp.concatenate([vp[:, j:j + nq] for j in range(nb + 1)], axis=2)
    qb = jnp.pad(q, ((0, 0), (0, lp - L), (0, 0), (0, 0))).reshape(n, nq, Q_BLOCK, hkv, g, hd)
    s = jnp.einsum('nbqhgd,nbkhd->nbhgqk', qb, kb, preferred_element_type=F32) * hd ** -0.5
    r = jnp.arange(Q_BLOCK)[:, None]
    c = jnp.arange(kw)[None, :]
    dist = r + nb * Q_BLOCK - c
    kpos = (jnp.arange(nq)[:, None, None] - nb) * Q_BLOCK + c[None]
    mask = (dist >= 0) & (dist <= window) & (kpos >= 0)
    bias = rp_bias(bias_tab, jnp.maximum(dist, 0) * dist_scale)
    bias = bias.transpose(2, 0, 1).reshape(hkv, g, Q_BLOCK, kw)
    s = jnp.where(mask[None, :, None, None], s + bias, NEG)
    m = s.max(-1)
    p = jnp.exp(s - m[..., None])
    l = p.sum(-1)
    acc = jnp.einsum('nbhgqk,nbkhd->nbqhgd', p, vb.astype(F32))
    m = m.transpose(0, 1, 4, 2, 3).reshape(n, lp, hq)[:, :L]
    l = l.transpose(0, 1, 4, 2, 3).reshape(n, lp, hq)[:, :L]
    acc = acc.reshape(n, lp, hq, hd)[:, :L]
    return m, l, acc


def dilated_stats(q, k, v, window, dil, bias_tab):
    n, L = q.shape[:2]

    def sub(x):
        return x.reshape(n, L // dil, dil, *x.shape[2:]).swapaxes(1, 2).reshape(n * dil, L // dil, *x.shape[2:])

    def unsub(x):
        return x.reshape(n, dil, L // dil, *x.shape[2:]).swapaxes(1, 2).reshape(n, L, *x.shape[2:])

    m, l, acc = banded_stats(sub(q), sub(k), sub(v), window // dil, bias_tab, dil)
    return unsub(m), unsub(l), unsub(acc)


def strided_window_stats(q, buf, new, window, dil, bias_tab):
    n, t, hq, hd = q.shape
    hkv = new.shape[3]
    g = hq // hkv
    wb = buf.shape[1]
    kv = jnp.concatenate([buf, new.astype(buf.dtype)], axis=1)
    j = jnp.arange(window // dil + 1)
    idx = wb + jnp.arange(t)[:, None] - dil * j[None, :]
    valid = idx >= 0
    kvg = kv[:, jnp.maximum(idx, 0)]
    qg = q.reshape(n, t, hkv, g, hd)
    s = jnp.einsum('nthgd,ntkhd->nthgk', qg, kvg[:, :, :, 0], preferred_element_type=F32) * hd ** -0.5
    bias = rp_bias(bias_tab, dil * j).T.reshape(hkv, g, -1)
    s = jnp.where(valid[None, :, None, None, :], s + bias, NEG)
    m = s.max(-1)
    p = jnp.exp(s - m[..., None])
    l = p.sum(-1)
    acc = jnp.einsum('nthgk,ntkhd->nthgd', p, kvg[:, :, :, 1].astype(F32))
    return m.reshape(n, t, hq), l.reshape(n, t, hq), acc.reshape(n, t, hq, hd)


def compress_kv(kv, w1, w2, pe):
    n, L = kv.shape[:2]
    nch = L // CMP_STRIDE
    z = kv[:, :nch * CMP_STRIDE].reshape(n, nch, CMP_STRIDE, *kv.shape[2:])
    ha = jnp.einsum('ncsphd,psde->ncphe', z, w1[:, :CMP_STRIDE])
    hb = jnp.einsum('ncsphd,psde->ncphe', z, w1[:, CMP_STRIDE:])
    pe_h = jnp.einsum('psd,psde->pe', pe, w1)
    hid = jax.nn.gelu(ha[:, :-1] + hb[:, 1:] + pe_h[:, None, :])
    out = jnp.einsum('ncphe,ped->ncphd', hid, w2)
    cend = jnp.arange(nch - 1) * CMP_STRIDE + CMP_LEN - 1
    return out, cend


def nsa_cmp_slc(q, qpos, ckv, cend, gather_fn, n_slc, bias_tab):
    n, t, hq, hd = q.shape
    hkv = ckv.shape[3]
    g = hq // hkv
    scale = hd ** -0.5
    qg = q.reshape(n, t, hkv, g, hd)
    cdist = qpos[:, None] - cend[None, :]
    cmask = (cdist >= 0)[:, None, None, :]
    cbias = rp_bias(bias_tab, jnp.maximum(cdist, 0)).transpose(0, 2, 1).reshape(t, hkv, g, -1)
    s = jnp.einsum('nthgd,nchd->nthgc', qg, ckv[:, :, 0], preferred_element_type=F32) * scale + cbias
    s = jnp.where(cmask, s, NEG)
    p = jnp.where(cmask, jnp.exp(s - s.max(-1, keepdims=True)), 0.0)
    p = p / jnp.maximum(p.sum(-1, keepdims=True), 1e-30)
    o_cmp = jnp.einsum('nthgc,nchd->nthgd', p, ckv[:, :, 1].astype(F32))
    cstart = cend - CMP_LEN + 1
    jb = jnp.arange(n_slc)
    overlap = ((cstart[:, None] <= jb[None, :] * SLC_BLOCK + SLC_BLOCK - 1)
               & (cend[:, None] >= jb[None, :] * SLC_BLOCK)).astype(F32)
    imp = jnp.einsum('nthgc,cj->nthj', p, overlap)
    cur = qpos[:, None] // SLC_BLOCK
    valid = (jb[None, :] * SLC_BLOCK <= qpos[:, None])[:, None, :]
    forced = ((jb[None, :] == 0) | (jb[None, :] == cur) | (jb[None, :] == cur - 1))[:, None, :]
    imp = jnp.where(valid, imp + jnp.where(forced, FORCE, 0.0), NEG)
    k_sel = min(N_SELECT, n_slc)
    _, idx = lax.top_k(imp, k_sel)
    blk = gather_fn(idx)
    kpos = idx[..., None] * SLC_BLOCK + jnp.arange(SLC_BLOCK)
    sdist = qpos[None, :, None, None, None] - kpos
    tab = bias_tab.astype(F32).reshape(RP_BUCKETS, hkv, g)
    sbias = tab[t5_bucket(jnp.maximum(sdist, 0)), jnp.arange(hkv)[None, None, :, None, None]]
    sbias = jnp.moveaxis(sbias, -1, 3)
    s2 = jnp.einsum('nthgd,nthksd->nthgks', qg, blk[..., 0, :], preferred_element_type=F32) * scale + sbias
    s2 = jnp.where((sdist >= 0)[:, :, :, None], s2, NEG).reshape(n, t, hkv, g, -1)
    p2 = jax.nn.softmax(s2, axis=-1).reshape(n, t, hkv, g, k_sel, SLC_BLOCK)
    o_slc = jnp.einsum('nthgks,nthksd->nthgd', p2, blk[..., 1, :].astype(F32))
    return o_cmp.reshape(n, t, hq, hd), o_slc.reshape(n, t, hq, hd)


def even_project(h, w_in):
    n, t, _ = h.shape
    qa, ka, va, qb, kb, vb = _split(h @ w_in, EVEN_SPLITS)
    qa = qa.reshape(n, t, A_HEADS, HEAD_DIM)
    kva = jnp.stack([ka.reshape(n, t, A_KV_HEADS, HEAD_DIM), va.reshape(n, t, A_KV_HEADS, HEAD_DIM)], axis=2)
    qb = qb.reshape(n, t, B_HEADS, HEAD_DIM)
    kvb = jnp.stack([kb.reshape(n, t, B_HEADS, HEAD_DIM), vb.reshape(n, t, B_HEADS, HEAD_DIM)], axis=2)
    return qa, kva, qb, kvb


def even_prompt(h, w_in, w_out, sink, rel_bias):
    n, s_len, _ = h.shape
    qa, kva, qb, kvb = even_project(h, w_in)
    oa = merge_stats([banded_stats(qa, kva[:, :, 0], kva[:, :, 1], WIN_A, rel_bias[:, :A_HEADS], 1)], sink)
    ob = merge_stats([dilated_stats(qb, kvb[:, :, 0], kvb[:, :, 1], w, d, rel_bias[:, A_HEADS:])
                      for (w, d) in DIL_PAIRS])
    o = jnp.concatenate([oa, ob], axis=2).reshape(n, s_len, D_MODEL).astype(h.dtype)
    state_a = kva[:, s_len - min(WIN_A, s_len):]
    state_b = kvb[:, s_len - min(DIL_MAX_WIN, s_len):]
    return o @ w_out, (state_a, state_b)


def even_sample(h, buf_a, buf_b, w_in, w_out, sink, rel_bias):
    n, t, _ = h.shape
    qa, kva, qb, kvb = even_project(h, w_in)
    oa = merge_stats([strided_window_stats(qa, buf_a, kva, WIN_A, 1, rel_bias[:, :A_HEADS])], sink)
    ob = merge_stats([strided_window_stats(qb, buf_b, kvb, w, d, rel_bias[:, A_HEADS:]) for (w, d) in DIL_PAIRS])
    o = jnp.concatenate([oa, ob], axis=2).reshape(n, t, D_MODEL).astype(h.dtype)
    new_a = jnp.concatenate([buf_a, kva.astype(buf_a.dtype)], axis=1)[:, t:]
    new_b = jnp.concatenate([buf_b, kvb.astype(buf_b.dtype)], axis=1)[:, t:]
    return o @ w_out, (new_a, new_b)


def nsa_project(h, w_in):
    n, t, _ = h.shape
    q, kv, gt = _split(h @ w_in, NSA_SPLITS)
    q = q.reshape(n, t, NSA_HEADS, HEAD_DIM)
    kv = kv.reshape(n, t, 3, 2, NSA_KV_HEADS, HEAD_DIM)
    gt = jax.nn.sigmoid(gt.astype(F32)).reshape(n, t, NSA_HEADS, 3)
    return q, kv[:, :, 0], kv[:, :, 1], kv[:, :, 2], gt


def nsa_combine(h, o_cmp, o_slc, o_win, gt, w_out):
    n, t, _ = h.shape
    o = gt[..., 0:1] * o_cmp + gt[..., 1:2] * o_slc + gt[..., 2:3] * o_win
    return o.reshape(n, t, D_MODEL).astype(h.dtype) @ w_out


def nsa_prompt(h, w_in, w_out, cw1, cw2, cpe, rel_bias):
    n, s_len, _ = h.shape
    q, kv_cmp, kv_slc, kv_win, gt = nsa_project(h, w_in)
    ckv, cend = compress_kv(kv_cmp, cw1, cw2, cpe)
    n_slc = -(-s_len // SLC_BLOCK)
    blocks = jnp.pad(kv_slc, ((0, 0), (0, n_slc * SLC_BLOCK - s_len), (0, 0), (0, 0), (0, 0)))
    blocks = blocks.reshape(n, n_slc, SLC_BLOCK, 2, NSA_KV_HEADS, HEAD_DIM)
    b_idx = jnp.arange(n)[:, None, None, None]
    h_idx = jnp.arange(NSA_KV_HEADS)[None, None, :, None]

    def gather_fn(idx):
        return blocks[b_idx, idx, :, :, h_idx]

    nqb = s_len // Q_BLOCK
    qs = q.reshape(n, nqb, Q_BLOCK, NSA_HEADS, HEAD_DIM).swapaxes(0, 1)
    starts = jnp.arange(nqb, dtype=jnp.int32) * Q_BLOCK

    def one_block(args):
        qb, t0 = args
        return nsa_cmp_slc(qb, t0 + jnp.arange(Q_BLOCK, dtype=jnp.int32), ckv, cend, gather_fn, n_slc, rel_bias)

    o_cmp, o_slc = lax.map(one_block, (qs, starts))
    o_cmp = o_cmp.swapaxes(0, 1).reshape(n, s_len, NSA_HEADS, HEAD_DIM)
    o_slc = o_slc.swapaxes(0, 1).reshape(n, s_len, NSA_HEADS, HEAD_DIM)
    o_win = merge_stats([banded_stats(q, kv_win[:, :, 0], kv_win[:, :, 1], WIN_C, rel_bias, 1)])
    y = nsa_combine(h, o_cmp, o_slc, o_win, gt, w_out)
    return y, (kv_cmp, kv_slc, kv_win[:, s_len - min(WIN_C, s_len):])


def nsa_sample(h, pool_cmp, pool_slc, win_buf, page_table, w_in, w_out, cw1, cw2, cpe, rel_bias):
    n, t, _ = h.shape
    q, kv_cmp, kv_slc, kv_win, gt = nsa_project(h, w_in)
    past = page_table.shape[1] * PAGE_SIZE
    past_cmp = pool_cmp[page_table].reshape(n, past, 2, NSA_KV_HEADS, HEAD_DIM)
    ckv, cend = compress_kv(jnp.concatenate([past_cmp, kv_cmp.astype(past_cmp.dtype)], axis=1), cw1, cw2, cpe)
    n_slc = -(-(past + t) // SLC_BLOCK)
    nb_past = past // SLC_BLOCK
    n_new = n_slc - nb_past
    bpp = PAGE_SIZE // SLC_BLOCK
    new_blocks = jnp.pad(kv_slc, ((0, 0), (0, n_new * SLC_BLOCK - t), (0, 0), (0, 0), (0, 0)))
    new_blocks = new_blocks.reshape(n, n_new, SLC_BLOCK, 2, NSA_KV_HEADS, HEAD_DIM)
    b_idx = jnp.arange(n)[:, None, None, None]
    h_idx = jnp.arange(NSA_KV_HEADS)[None, None, :, None]

    def gather_fn(idx):
        pidx = jnp.clip(idx, 0, nb_past - 1)
        page = page_table[b_idx, pidx // bpp]
        rows = ((pidx % bpp) * SLC_BLOCK)[..., None] + jnp.arange(SLC_BLOCK)
        from_past = pool_slc[page[..., None], rows, :, h_idx[..., None]]
        from_new = new_blocks[b_idx, jnp.clip(idx - nb_past, 0, n_new - 1), :, :, h_idx]
        return jnp.where((idx < nb_past)[..., None, None, None], from_past, from_new.astype(from_past.dtype))

    qpos = past + jnp.arange(t, dtype=jnp.int32)
    o_cmp, o_slc = nsa_cmp_slc(q, qpos, ckv, cend, gather_fn, n_slc, rel_bias)
    o_win = merge_stats([strided_window_stats(q, win_buf, kv_win, WIN_C, 1, rel_bias)])
    y = nsa_combine(h, o_cmp, o_slc, o_win, gt, w_out)
    new_win = jnp.concatenate([win_buf, kv_win.astype(win_buf.dtype)], axis=1)[:, t:]
    return y, (kv_cmp, kv_slc, new_win)


def sandwich_block(x, c, mixer, ada_w, ada_b, norm_g, w_up, w_down):
    n, _, d = x.shape
    mod = (jax.nn.silu(c) @ ada_w + ada_b).reshape(n, 6, d)[:, :, None, :]
    sh1, sc1, gt1, sh2, sc2, gt2 = [mod[:, i] for i in range(6)]
    h = rmsnorm(x, norm_g[0]) * (1 + sc1) + sh1
    y, state = mixer(h)
    x = x + gt1 * rmsnorm(y, norm_g[1])
    h = rmsnorm(x, norm_g[2]) * (1 + sc2) + sh2
    u = jnp.square(jax.nn.relu(h @ w_up)) @ w_down
    x = x + gt2 * rmsnorm(u, norm_g[3])
    return x, state


def setup_inputs(seed: int = 0) -> dict:
    key = jax.random.key(seed)
    ks = iter(jax.random.split(key, 32))

    def nrm(shape, scale):
        return jax.random.normal(next(ks), shape, F32) * scale

    n_pages = PAST_LEN // PAGE_SIZE
    n_pool = (DEC_BATCH * n_pages * 5) // 4
    hd = HEAD_DIM
    page_table = jax.random.permutation(next(ks), n_pool)[:DEC_BATCH * n_pages]
    page_table = page_table.reshape(DEC_BATCH, n_pages).astype(jnp.int32)
    return {
        'x_prompt': nrm((BATCH, SEQ, D_MODEL), 1.0),
        'x_sample': nrm((DEC_BATCH, DEC_SEQ, D_MODEL), 1.0),
        'cache_swa_kv': nrm((N_EVEN, DEC_BATCH, min(WIN_A, PAST_LEN), 2, A_KV_HEADS, hd), 1.0),
        'cache_dil_kv': nrm((N_EVEN, DEC_BATCH, min(DIL_MAX_WIN, PAST_LEN), 2, B_HEADS, hd), 1.0),
        'cache_nsa_cmp': nrm((N_ODD, n_pool, PAGE_SIZE, 2, NSA_KV_HEADS, hd), 1.0),
        'cache_nsa_slc': nrm((N_ODD, n_pool, PAGE_SIZE, 2, NSA_KV_HEADS, hd), 1.0),
        'cache_nsa_win': nrm((N_ODD, DEC_BATCH, min(WIN_C, PAST_LEN), 2, NSA_KV_HEADS, hd), 1.0),
        'page_table': page_table,
        'c_prompt': nrm((BATCH, D_MODEL), 1.0),
        'c_sample': nrm((DEC_BATCH, D_MODEL), 1.0),
        'rel_bias': nrm((RP_BUCKETS, N_HEADS), 0.1),
        'ada_w': nrm((DEPTH, D_MODEL, 6 * D_MODEL), 0.5 * D_MODEL ** -0.5),
        'ada_b': nrm((DEPTH, 6 * D_MODEL), 0.01),
        'norm_g': 1.0 + nrm((DEPTH, 4, D_MODEL), 0.02),
        'mlp_up': nrm((DEPTH, D_MODEL, D_FF), D_MODEL ** -0.5),
        'mlp_down': nrm((DEPTH, D_FF, D_MODEL), D_FF ** -0.5),
        'even_w_in': nrm((N_EVEN, D_MODEL, sum(EVEN_SPLITS)), D_MODEL ** -0.5),
        'even_w_out': nrm((N_EVEN, D_MODEL, D_MODEL), D_MODEL ** -0.5),
        'attn_sinks': nrm((N_EVEN, A_HEADS), 0.5),
        'nsa_w_in': nrm((N_ODD, D_MODEL, sum(NSA_SPLITS)), D_MODEL ** -0.5),
        'nsa_w_out': nrm((N_ODD, D_MODEL, D_MODEL), D_MODEL ** -0.5),
        'cmp_w1': nrm((N_ODD, 2, CMP_LEN, hd, CMP_HID), (CMP_LEN * hd) ** -0.5),
        'cmp_w2': nrm((N_ODD, 2, CMP_HID, hd), CMP_HID ** -0.5),
        'cmp_pe': nrm((N_ODD, 2, CMP_LEN, hd), 0.1),
    }


def reference(x_prompt, x_sample, cache_swa_kv, cache_dil_kv, cache_nsa_cmp, cache_nsa_slc, cache_nsa_win,
              page_table, c_prompt, c_sample, rel_bias, ada_w, ada_b, norm_g, mlp_up, mlp_down,
              even_w_in, even_w_out, attn_sinks, nsa_w_in, nsa_w_out, cmp_w1, cmp_w2, cmp_pe):
    xp, xs = x_prompt, x_sample
    swa_p, swa_s, dil_p, dil_s = [], [], [], []
    cmp_p, cmp_s, slc_p, slc_s, win_p, win_s = [], [], [], [], [], []
    for layer in range(DEPTH):
        li = layer // 2
        ffn = (ada_w[layer], ada_b[layer], norm_g[layer], mlp_up[layer], mlp_down[layer])
        if layer % 2 == 0:
            wts = (even_w_in[li], even_w_out[li], attn_sinks[li], rel_bias)
            xp, (sa, sb) = sandwich_block(xp, c_prompt, lambda h: even_prompt(h, *wts), *ffn)
            xs, (ta, tb) = sandwich_block(
                xs, c_sample, lambda h: even_sample(h, cache_swa_kv[li], cache_dil_kv[li], *wts), *ffn)
            swa_p.append(sa)
            dil_p.append(sb)
            swa_s.append(ta)
            dil_s.append(tb)
        else:
            wts = (nsa_w_in[li], nsa_w_out[li], cmp_w1[li], cmp_w2[li], cmp_pe[li], rel_bias)
            xp, (pc, ps, pw) = sandwich_block(xp, c_prompt, lambda h: nsa_prompt(h, *wts), *ffn)
            xs, (qc, qs_, qw) = sandwich_block(
                xs, c_sample,
                lambda h: nsa_sample(h, cache_nsa_cmp[li], cache_nsa_slc[li], cache_nsa_win[li], page_table, *wts),
                *ffn)
            cmp_p.append(pc)
            slc_p.append(ps)
            win_p.append(pw)
            cmp_s.append(qc)
            slc_s.append(qs_)
            win_s.append(qw)
    return (xp, xs, jnp.stack(swa_p), jnp.stack(swa_s), jnp.stack(dil_p), jnp.stack(dil_s),
            jnp.stack(cmp_p), jnp.stack(cmp_s), jnp.stack(slc_p), jnp.stack(slc_s),
            jnp.stack(win_p), jnp.stack(win_s))
```

```python
import functools
import math

import jax
import jax.numpy as jnp
from jax import lax
from jax.experimental import pallas as pl
from jax.experimental.pallas import tpu as pltpu

F32 = jnp.float32
BF16 = jnp.bfloat16

HEAD_DIM = 64
A_HEADS = 8
A_KV_HEADS = 2
B_HEADS = 8
WIN_A = 128
DIL_PAIRS = ((128, 1), (512, 4), (2048, 16))
DIL_MAX_WIN = 2048
NSA_HEADS = 16
NSA_KV_HEADS = 2
CMP_STRIDE = 16
CMP_LEN = 32
CMP_HID = 128
SLC_BLOCK = 64
N_SELECT = 16
WIN_C = 512
RP_BUCKETS = 32
RP_MAX_DIST = 2048
Q_BLOCK = 128
EPS = 1e-6
NEG = -1e30
REMOVED = -3e38
FORCE = 1e6
SCALE = HEAD_DIM ** -0.5

VMEM_LIMIT_BYTES = 56 * 1024 * 1024
LANES = 128


def _cparams(*sem):
    return pltpu.CompilerParams(dimension_semantics=sem, vmem_limit_bytes=VMEM_LIMIT_BYTES)


def _round_up(x, m):
    return (x + m - 1) // m * m


def _t5_bucket(dist):
    n = dist.astype(jnp.int32)
    exact = RP_BUCKETS // 2
    val = jnp.log(jnp.maximum(n, 1).astype(F32) / exact) / math.log(RP_MAX_DIST / exact)
    large = jnp.minimum(exact + (val * (RP_BUCKETS - exact)).astype(jnp.int32), RP_BUCKETS - 1)
    return jnp.where(n < exact, n, large)


def _bias_of(tab, dist):
    b = tab[_t5_bucket(jnp.maximum(dist, 0))].astype(F32)
    return jnp.moveaxis(b, -1, 0)


def _rms(x, g):
    return x * lax.rsqrt(jnp.mean(x * x, axis=-1, keepdims=True) + EPS) * g


def _dot(a, b):
    return jnp.dot(a, b, preferred_element_type=F32)


def _dot_nt(a, b):
    return lax.dot_general(a, b, (((1,), (1,)), ((), ())), preferred_element_type=F32)


def _split_bf16(x):
    hi = x.astype(BF16)
    lo = (x - hi.astype(F32)).astype(BF16)
    return hi, lo


def _dot_nt_3pass(a, b):
    ah, al = _split_bf16(a)
    bh, bl = _split_bf16(b)
    return _dot_nt(ah, bh) + (_dot_nt(ah, bl) + _dot_nt(al, bh))


def _dot_2pass_lhs(a, b_bf16):
    ah, al = _split_bf16(a)
    return _dot(ah, b_bf16) + _dot(al, b_bf16)


def _gelu_tanh(x):
    return 0.5 * x * (1.0 + jnp.tanh(math.sqrt(2.0 / math.pi) * (x + 0.044715 * (x * x * x))))


def _linear_kernel(*refs, pre, has_bias):
    if has_bias:
        a_ref, b_ref, bias_ref, o_ref = refs
    else:
        a_ref, b_ref, o_ref = refs
    a = a_ref[...]
    if pre == "silu":
        a = a * jax.nn.sigmoid(a)
    acc = _dot(a.astype(BF16), b_ref[...])
    if has_bias:
        acc = acc + bias_ref[...]
    o_ref[...] = acc


def _linear(a, b, layer, bias=None, *, pre=None, tm, tn):
    m, k = a.shape
    n = b.shape[2]
    assert m % tm == 0 and n % tn == 0
    in_specs = [pl.BlockSpec((tm, k), lambda i, j: (i, 0)),
                pl.BlockSpec((None, k, tn), lambda i, j: (layer, 0, j))]
    args = [a, b]
    if bias is not None:
        in_specs.append(pl.BlockSpec((None, 1, tn), lambda i, j: (layer, 0, j)))
        args.append(bias)
    return pl.pallas_call(
        functools.partial(_linear_kernel, pre=pre, has_bias=bias is not None),
        grid=(m // tm, n // tn),
        in_specs=in_specs,
        out_specs=pl.BlockSpec((tm, tn), lambda i, j: (i, j)),
        out_shape=jax.ShapeDtypeStruct((m, n), F32),
        compiler_params=_cparams("parallel", "parallel"),
        name="linear",
    )(*args)


def _norm_linear_kernel(x_ref, g_ref, sc_ref, sh_ref, w_ref, *o_refs, splits):
    h = _rms(x_ref[...], g_ref[...]) * (1.0 + sc_ref[...]) + sh_ref[...]
    hb = h.astype(BF16)
    off = 0
    for o_ref, sz in zip(o_refs, splits):
        o_ref[...] = _dot(hb, w_ref[:, off:off + sz])
        off += sz


def _norm_linear(x, g, sc, sh, w, layer, splits, *, tm):
    b, l, d = x.shape
    n = w.shape[2]
    assert sum(splits) == n and l % tm == 0
    r = sc.shape[1]
    rb = 1 if r == 1 else tm
    mod_map = (lambda bi, i: (bi, 0, 0)) if r == 1 else (lambda bi, i: (bi, i, 0))
    return pl.pallas_call(
        functools.partial(_norm_linear_kernel, splits=splits),
        grid=(b, l // tm),
        in_specs=[pl.BlockSpec((None, tm, d), lambda bi, i: (bi, i, 0)),
                  pl.BlockSpec((1, d), lambda bi, i: (0, 0)),
                  pl.BlockSpec((None, rb, d), mod_map),
                  pl.BlockSpec((None, rb, d), mod_map),
                  pl.BlockSpec((None, d, n), lambda bi, i: (layer, 0, 0))],
        out_specs=[pl.BlockSpec((None, tm, sz), lambda bi, i: (bi, i, 0)) for sz in splits],
        out_shape=[jax.ShapeDtypeStruct((b, l, sz), F32) for sz in splits],
        compiler_params=_cparams("parallel", "parallel"),
        name="norm_linear",
    )(x, g, sc, sh, w)


def _banded_kernel(*refs, hq, hkv, nb, window, mode, has_sink, gate_col):
    it = iter(refs)
    q_ref = next(it)
    k_refs = [next(it) for _ in range(nb + 1)]
    v_refs = [next(it) for _ in range(nb + 1)]
    bias_ref = next(it)
    sink_ref = next(it) if has_sink else None
    gate_ref = next(it) if gate_col is not None else None
    outs = list(it)
    g = hq // hkv
    kw = (nb + 1) * Q_BLOCK
    i = pl.program_id(2)

    r = lax.broadcasted_iota(jnp.int32, (Q_BLOCK, kw), 0)
    c = lax.broadcasted_iota(jnp.int32, (Q_BLOCK, kw), 1)
    dist = r + nb * Q_BLOCK - c
    kpos = (i - nb) * Q_BLOCK + c
    mask = (dist >= 0) & (dist <= window) & (kpos >= 0)

    q = q_ref[...] * SCALE
    kcat = jnp.concatenate([kr[...] for kr in k_refs], axis=0).astype(BF16)
    vcat = jnp.concatenate([vr[...] for vr in v_refs], axis=0).astype(BF16)
    if gate_ref is not None:
        gates = jax.nn.sigmoid(gate_ref[...])
    for h in range(hq):
        kh = h // g
        hs = slice(h * HEAD_DIM, (h + 1) * HEAD_DIM)
        ks = slice(kh * HEAD_DIM, (kh + 1) * HEAD_DIM)
        s = _dot_nt(q[:, hs].astype(BF16), kcat[:, ks]) + bias_ref[h]
        s = jnp.where(mask, s, NEG)
        m = jnp.max(s, axis=-1, keepdims=True)
        p = jnp.exp(s - m)
        l = jnp.sum(p, axis=-1, keepdims=True)
        acc = _dot(p.astype(BF16), vcat[:, ks])
        if mode == "stats":
            acc_ref, m_ref, l_ref = outs
            acc_ref[:, hs] = acc
            m_ref[:, hs] = jnp.broadcast_to(m, (Q_BLOCK, HEAD_DIM))
            l_ref[:, hs] = jnp.broadcast_to(l, (Q_BLOCK, HEAD_DIM))
        else:
            (o_ref,) = outs
            if has_sink:
                sk = sink_ref[h]
                big = jnp.maximum(m, sk)
                w = jnp.exp(m - big)
                o = (acc * w) / (l * w + jnp.exp(sk - big))
            else:
                o = acc / l
            if gate_ref is not None:
                col = h * 3 + gate_col
                o = o * gates[:, col:col + 1]
            o_ref[:, hs] = o


def _banded_attn(q, kv, bias, *, hq, hkv, window, dil=1, mode="norm", sink=None,
                 gate=None, gate_col=None):
    b, l, wq = q.shape
    wk = hkv * HEAD_DIM
    assert wq == hq * HEAD_DIM and kv.shape == (b, l, 2 * wk)
    ls = l // dil
    assert l % dil == 0 and ls % Q_BLOCK == 0
    nq = ls // Q_BLOCK
    nb = -(-window // Q_BLOCK)
    qv = q.reshape(b, ls, dil * wq)
    kvv = kv.reshape(b, ls, dil * 2 * wk)

    def kmap(j, half):
        return lambda bi, r, i: (bi, jnp.maximum(i - nb + j, 0), 2 * r + half)

    in_specs = [pl.BlockSpec((None, Q_BLOCK, wq), lambda bi, r, i: (bi, i, r))]
    args = [qv]
    for half in (0, 1):
        for j in range(nb + 1):
            in_specs.append(pl.BlockSpec((None, Q_BLOCK, wk), kmap(j, half)))
            args.append(kvv)
    in_specs.append(pl.BlockSpec(bias.shape, lambda bi, r, i: (0, 0, 0)))
    args.append(bias)
    if sink is not None:
        in_specs.append(pl.BlockSpec(memory_space=pltpu.SMEM))
        args.append(sink)
    if gate is not None:
        assert dil == 1
        in_specs.append(pl.BlockSpec((None, Q_BLOCK, gate.shape[2]), lambda bi, r, i: (bi, i, 0)))
        args.append(gate)
    n_out = 3 if mode == "stats" else 1
    out = pl.pallas_call(
        functools.partial(_banded_kernel, hq=hq, hkv=hkv, nb=nb, window=window, mode=mode,
                          has_sink=sink is not None, gate_col=gate_col if gate is not None else None),
        grid=(b, dil, nq),
        in_specs=in_specs,
        out_specs=[pl.BlockSpec((None, Q_BLOCK, wq), lambda bi, r, i: (bi, i, r))] * n_out,
        out_shape=[jax.ShapeDtypeStruct((b, ls, dil * wq), F32)] * n_out,
        compiler_params=_cparams("parallel", "parallel", "parallel"),
        name="banded_attn",
    )(*args)
    return [o.reshape(b, l, wq) for o in out]


def _banded_bias(tab, window, dist_scale):
    nb = -(-window // Q_BLOCK)
    kw = (nb + 1) * Q_BLOCK
    r = jnp.arange(Q_BLOCK)[:, None]
    c = jnp.arange(kw)[None, :]
    return _bias_of(tab, jnp.maximum(r + nb * Q_BLOCK - c, 0) * dist_scale)


def _compress_finish_kernel(hab_ref, pe_ref, w2_ref, o_ref, *, nch, pad):
    hid_w = hab_ref.shape[1] // 2
    hab = hab_ref[...]
    ha = hab[:, :hid_w]
    hb_next = pltpu.roll(hab[:, hid_w:], nch - 1, axis=0)
    pe = pe_ref[0:1, :hid_w] + pe_ref[1:2, hid_w:]
    hid = _gelu_tanh(ha + hb_next + pe)
    ckv = _dot(hid.astype(BF16), w2_ref[...])
    if pad:
        o_ref[0:pad, :] = jnp.zeros((pad, o_ref.shape[1]), F32)
        o_ref[pad + nch:, :] = jnp.zeros((o_ref.shape[0] - pad - nch, o_ref.shape[1]), F32)
    o_ref[pad:pad + nch, :] = ckv


def _compress_finish(hab, pe_rows, w2big, layer, *, pad, rows_out):
    b, nch, w = hab.shape
    n_out = w2big.shape[2]
    return pl.pallas_call(
        functools.partial(_compress_finish_kernel, nch=nch, pad=pad),
        grid=(b,),
        in_specs=[pl.BlockSpec((None, nch, w), lambda bi: (bi, 0, 0)),
                  pl.BlockSpec(pe_rows.shape, lambda bi: (0, 0)),
                  pl.BlockSpec((None,) + w2big.shape[1:], lambda bi: (layer, 0, 0))],
        out_specs=pl.BlockSpec((None, rows_out, n_out), lambda bi: (bi, 0, 0)),
        out_shape=jax.ShapeDtypeStruct((b, rows_out, n_out), F32),
        compiler_params=_cparams("parallel"),
        name="compress_finish",
    )(hab, pe_rows, w2big)


def _compress_weights(cmp_w1, cmp_w2):
    lyr = cmp_w1.shape[0]
    eye_h = jnp.eye(NSA_KV_HEADS, dtype=F32)
    eye_p = jnp.eye(2, dtype=F32)
    w1 = cmp_w1.reshape(lyr, 2, 2, CMP_STRIDE, HEAD_DIM, CMP_HID)
    w1big = jnp.einsum("lpasde,pq,hk->lsphdaqke", w1, eye_p, eye_h)
    w1big = w1big.reshape(lyr, CMP_STRIDE * 2 * NSA_KV_HEADS * HEAD_DIM, 2 * 2 * NSA_KV_HEADS * CMP_HID)
    w2big = jnp.einsum("lped,pq,hk->lpheqkd", cmp_w2, eye_p, eye_h)
    w2big = w2big.reshape(lyr, 2 * NSA_KV_HEADS * CMP_HID, 2 * NSA_KV_HEADS * HEAD_DIM)
    return w1big.astype(BF16), w2big.astype(BF16)


def _pe_rows(cmp_pe):
    lyr = cmp_pe.shape[0]
    pe = cmp_pe.reshape(lyr, 2, 2, CMP_STRIDE, HEAD_DIM)
    pe = jnp.broadcast_to(pe[:, :, :, :, None, :], (lyr, 2, 2, CMP_STRIDE, NSA_KV_HEADS, HEAD_DIM))
    pe = pe.transpose(0, 2, 3, 1, 4, 5).reshape(lyr, 2, CMP_STRIDE * 2 * NSA_KV_HEADS * HEAD_DIM)
    return jnp.pad(pe, ((0, 0), (0, 6), (0, 0)))


def _overlap_table(nch, n_slc_pad):
    c = jnp.arange(nch)[:, None]
    jb = jnp.arange(n_slc_pad)[None, :]
    cstart = c * CMP_STRIDE
    cend = cstart + CMP_LEN - 1
    ov = (cstart <= jb * SLC_BLOCK + SLC_BLOCK - 1) & (cend >= jb * SLC_BLOCK) & (c < nch - 1)
    return ov.astype(BF16)


def _topk_select(imp, k_sel):
    lane = lax.broadcasted_iota(jnp.int32, imp.shape, 1).astype(F32)
    sel = jnp.zeros(imp.shape, F32)
    work = imp
    picks = []
    for _ in range(k_sel):
        mx = jnp.max(work, axis=-1, keepdims=True)
        idx = jnp.min(jnp.where(work == mx, lane, float(imp.shape[1])), axis=-1, keepdims=True)
        pick = lane == idx
        sel = jnp.where(pick, 1.0, sel)
        work = jnp.where(pick, REMOVED, work)
        picks.append(idx)
    return sel, picks


def _cmp_prompt_kernel(q_ref, ckv_ref, bias_ref, ov_ref, gate_ref, o_ref, sel_ref, *, nch, n_slc):
    i = pl.program_id(1)
    g = NSA_HEADS // NSA_KV_HEADS
    wk = NSA_KV_HEADS * HEAD_DIM
    start = pl.multiple_of(i * (Q_BLOCK // CMP_STRIDE), 8)
    win = ckv_ref[pl.ds(start, nch), :]
    ov = ov_ref[pl.ds(start, nch), :]
    r = lax.broadcasted_iota(jnp.int32, (Q_BLOCK, nch), 0)
    w = lax.broadcasted_iota(jnp.int32, (Q_BLOCK, nch), 1)
    e = w - (nch - Q_BLOCK // CMP_STRIDE)
    dist = r - CMP_STRIDE * e - (CMP_LEN - 1)
    cidx = e + i * (Q_BLOCK // CMP_STRIDE)
    mask = (dist >= 0) & (cidx >= 0)
    gates = jax.nn.sigmoid(gate_ref[...])
    q = q_ref[...] * SCALE

    n_slc_pad = ov_ref.shape[1]
    jb = lax.broadcasted_iota(jnp.int32, (Q_BLOCK, n_slc_pad), 1)
    qpos = i * Q_BLOCK + lax.broadcasted_iota(jnp.int32, (Q_BLOCK, n_slc_pad), 0)
    cur = qpos // SLC_BLOCK
    valid = (jb * SLC_BLOCK <= qpos) & (jb < n_slc)
    forced = (jb == 0) | (jb == cur) | (jb == cur - 1)

    for kh in range(NSA_KV_HEADS):
        k_h = win[:, kh * HEAD_DIM:(kh + 1) * HEAD_DIM]
        v_h = win[:, wk + kh * HEAD_DIM: wk + (kh + 1) * HEAD_DIM].astype(BF16)
        psum = jnp.zeros((Q_BLOCK, nch), F32)
        for gi in range(g):
            h = kh * g + gi
            hs = slice(h * HEAD_DIM, (h + 1) * HEAD_DIM)
            s = _dot_nt_3pass(q[:, hs], k_h) + bias_ref[h]
            s = jnp.where(mask, s, NEG)
            m = jnp.max(s, axis=-1, keepdims=True)
            p = jnp.where(mask, jnp.exp(s - m), 0.0)
            p = p / jnp.maximum(jnp.sum(p, axis=-1, keepdims=True), 1e-30)
            o = _dot(p.astype(BF16), v_h)
            o_ref[:, hs] = o * gates[:, h * 3:h * 3 + 1]
            psum = psum + p
        imp = _dot_2pass_lhs(psum, ov)
        imp = jnp.where(valid, imp + jnp.where(forced, FORCE, 0.0), NEG)
        sel, _ = _topk_select(imp, min(N_SELECT, n_slc))
        sel_ref[kh] = sel.astype(BF16)


def _cmp_prompt(q, ckv_pad, bias_rel, ov_pad, gate, *, nch, n_slc):
    b, l, wq = q.shape
    nq = l // Q_BLOCK
    n_slc_pad = ov_pad.shape[1]
    return pl.pallas_call(
        functools.partial(_cmp_prompt_kernel, nch=nch, n_slc=n_slc),
        grid=(b, nq),
        in_specs=[pl.BlockSpec((None, Q_BLOCK, wq), lambda bi, i: (bi, i, 0)),
                  pl.BlockSpec((None,) + ckv_pad.shape[1:], lambda bi, i: (bi, 0, 0)),
                  pl.BlockSpec(bias_rel.shape, lambda bi, i: (0, 0, 0)),
                  pl.BlockSpec(ov_pad.shape, lambda bi, i: (0, 0)),
                  pl.BlockSpec((None, Q_BLOCK, gate.shape[2]), lambda bi, i: (bi, i, 0))],
        out_specs=[pl.BlockSpec((None, Q_BLOCK, wq), lambda bi, i: (bi, i, 0)),
                   pl.BlockSpec((None, NSA_KV_HEADS, Q_BLOCK, n_slc_pad), lambda bi, i: (bi, 0, i, 0))],
        out_shape=[jax.ShapeDtypeStruct((b, l, wq), F32),
                   jax.ShapeDtypeStruct((b, NSA_KV_HEADS, l, n_slc_pad), BF16)],
        compiler_params=_cparams("parallel", "parallel"),
        name="cmp_prompt",
    )(q, ckv_pad, bias_rel, ov_pad, gate)


def _slc_prompt_kernel(q_ref, k_ref, v_ref, sel_ref, bias_ref, gate_ref, o_ref, qs_ref, *, nd):
    kvh = pl.program_id(1)
    i = pl.program_id(2)
    g = NSA_HEADS // NSA_KV_HEADS
    rows = g * Q_BLOCK
    q = q_ref[...] * SCALE
    for gi in range(g):
        qs_ref[gi * Q_BLOCK:(gi + 1) * Q_BLOCK, :] = q[:, gi * HEAD_DIM:(gi + 1) * HEAD_DIM].astype(BF16)
    sel = sel_ref[...]
    n_slc_pad = sel.shape[1]
    blk = lax.broadcasted_iota(jnp.int32, (n_slc_pad, Q_BLOCK), 0)
    col = lax.broadcasted_iota(jnp.int32, (n_slc_pad, Q_BLOCK), 1)
    rr = lax.broadcasted_iota(jnp.int32, (Q_BLOCK, Q_BLOCK), 0)
    cc = lax.broadcasted_iota(jnp.int32, (Q_BLOCK, Q_BLOCK), 1)
    bpt = Q_BLOCK // SLC_BLOCK

    def body(j, carry):
        m_prev, l_prev, acc_prev = carry
        row0 = pl.multiple_of(j * Q_BLOCK, Q_BLOCK)
        kt = k_ref[pl.ds(row0, Q_BLOCK), :]
        vt = v_ref[pl.ds(row0, Q_BLOCK), :]
        k_h = jnp.where(kvh == 0, kt[:, :HEAD_DIM], kt[:, HEAD_DIM:]).astype(BF16)
        v_h = jnp.where(kvh == 0, vt[:, :HEAD_DIM], vt[:, HEAD_DIM:]).astype(BF16)
        s = _dot_nt(qs_ref[...], k_h).reshape(g, Q_BLOCK, Q_BLOCK)
        s = s + bias_ref[jnp.minimum(i - j, nd)]
        expand = jnp.where(blk == j * bpt + col // SLC_BLOCK, 1.0, 0.0).astype(BF16)
        picked = _dot(sel, expand) > 0.5
        mask = (picked & ((cc <= rr) | (j < i)))[None]
        s = jnp.where(mask, s, NEG)
        m_new = jnp.maximum(m_prev, jnp.max(s, axis=-1, keepdims=True))
        alpha = jnp.exp(m_prev - m_new)
        p = jnp.where(mask, jnp.exp(s - m_new), 0.0)
        l_new = alpha * l_prev + jnp.sum(p, axis=-1, keepdims=True)
        pv = _dot(p.reshape(rows, Q_BLOCK).astype(BF16), v_h).reshape(g, Q_BLOCK, HEAD_DIM)
        return m_new, l_new, alpha * acc_prev + pv

    init = (jnp.full((g, Q_BLOCK, 1), NEG, F32), jnp.zeros((g, Q_BLOCK, 1), F32),
            jnp.zeros((g, Q_BLOCK, HEAD_DIM), F32))
    _, l, acc = lax.fori_loop(0, i + 1, body, init)
    o = (acc / l).reshape(rows, HEAD_DIM)
    gates = jax.nn.sigmoid(gate_ref[...])
    for gi in range(g):
        lane = lax.broadcasted_iota(jnp.int32, gates.shape, 1)
        gcol = jnp.sum(jnp.where(lane == (kvh * g + gi) * 3 + 1, gates, 0.0), axis=-1, keepdims=True)
        o_ref[:, gi * HEAD_DIM:(gi + 1) * HEAD_DIM] = o[gi * Q_BLOCK:(gi + 1) * Q_BLOCK, :] * gcol


def _slc_prompt(q, kv_slc, sel, bias_tiles, gate):
    b, l, wq = q.shape
    nq = l // Q_BLOCK
    g = NSA_HEADS // NSA_KV_HEADS
    wg = g * HEAD_DIM
    wk = NSA_KV_HEADS * HEAD_DIM
    nd = bias_tiles.shape[0] - 1
    n_slc_pad = sel.shape[3]
    return pl.pallas_call(
        functools.partial(_slc_prompt_kernel, nd=nd),
        grid=(b, NSA_KV_HEADS, nq),
        in_specs=[pl.BlockSpec((None, Q_BLOCK, wg), lambda bi, kh, i: (bi, i, kh)),
                  pl.BlockSpec((None, l, wk), lambda bi, kh, i: (bi, 0, 0)),
                  pl.BlockSpec((None, l, wk), lambda bi, kh, i: (bi, 0, 1)),
                  pl.BlockSpec((None, None, Q_BLOCK, n_slc_pad), lambda bi, kh, i: (bi, kh, i, 0)),
                  pl.BlockSpec((nd + 1, g, Q_BLOCK, Q_BLOCK), lambda bi, kh, i: (0, kh, 0, 0)),
                  pl.BlockSpec((None, Q_BLOCK, gate.shape[2]), lambda bi, kh, i: (bi, i, 0))],
        out_specs=pl.BlockSpec((None, Q_BLOCK, wg), lambda bi, kh, i: (bi, i, kh)),
        out_shape=jax.ShapeDtypeStruct((b, l, wq), F32),
        scratch_shapes=[pltpu.VMEM((g * Q_BLOCK, HEAD_DIM), BF16)],
        compiler_params=_cparams("parallel", "parallel", "arbitrary"),
        name="slc_prompt",
    )(q, kv_slc, kv_slc, sel, bias_tiles, gate)


def _slc_bias_tiles(tab, nq):
    nd = min(nq, 13)
    d = jnp.arange(nd)[:, None, None] * Q_BLOCK + jnp.arange(Q_BLOCK)[None, :, None] - jnp.arange(Q_BLOCK)[None, None, :]
    near = _bias_of(tab, d)
    far = jnp.broadcast_to(tab[RP_BUCKETS - 1].astype(F32)[:, None, None, None], (tab.shape[1], 1, Q_BLOCK, Q_BLOCK))
    return jnp.concatenate([near, far], axis=1).transpose(1, 0, 2, 3)


def _post_kernel(*refs, mode, n_attn, ff_chunk):
    it = iter(refs)
    x_ref = next(it)
    if mode == "even":
        oa_ref = next(it)
        stats = [(next(it), next(it), next(it)) for _ in range((n_attn - 1) // 3)]
    else:
        branch_refs = [next(it) for _ in range(n_attn)]
    wout_ref, g_ref, gt1_ref, sc2_ref, sh2_ref, gt2_ref, wup_ref, wdown_ref, o_ref = it

    if mode == "even":
        ms = [m_ref[...] for _, m_ref, _ in stats]
        big = functools.reduce(jnp.maximum, ms)
        l_tot = 0.0
        acc_tot = 0.0
        for (acc_ref, _, l_ref), m in zip(stats, ms):
            w = jnp.exp(m - big)
            l_tot = l_tot + l_ref[...] * w
            acc_tot = acc_tot + acc_ref[...] * w
        ob = acc_tot / l_tot
        o = jnp.concatenate([oa_ref[...], ob], axis=1)
    else:
        o = branch_refs[0][...]
        for br in branch_refs[1:]:
            o = o + br[...]

    y = _dot(o.astype(BF16), wout_ref[...])
    x1 = x_ref[...] + gt1_ref[...] * _rms(y, g_ref[1:2, :])
    h2 = (_rms(x1, g_ref[2:3, :]) * (1.0 + sc2_ref[...]) + sh2_ref[...]).astype(BF16)
    d_ff = wup_ref.shape[1]
    u = jnp.zeros(x1.shape, F32)
    for c0 in range(0, d_ff, ff_chunk):
        a = jnp.square(jnp.maximum(_dot(h2, wup_ref[:, c0:c0 + ff_chunk]), 0.0))
        u = u + _dot(a.astype(BF16), wdown_ref[c0:c0 + ff_chunk, :])
    o_ref[...] = x1 + gt2_ref[...] * _rms(u, g_ref[3:4, :])


def _post_block(x, attn, w_out, norm_g, mods, w_up, w_down, layer, *, mode, tm):
    b, l, d = x.shape
    r = mods[0].shape[1]
    rb = 1 if r == 1 else tm
    mod_map = (lambda bi, i: (bi, 0, 0)) if r == 1 else (lambda bi, i: (bi, i, 0))
    d_ff = w_up.shape[2]
    tok = lambda bi, i: (bi, i, 0)
    in_specs = [pl.BlockSpec((None, tm, d), tok)]
    in_specs += [pl.BlockSpec((None, tm, a.shape[2]), tok) for a in attn]
    in_specs += [pl.BlockSpec((None, d, d), lambda bi, i: (layer // 2, 0, 0)),
                 pl.BlockSpec((None, 4, d), lambda bi, i: (layer, 0, 0))]
    in_specs += [pl.BlockSpec((None, rb, d), mod_map)] * 4
    in_specs += [pl.BlockSpec((None, d, d_ff), lambda bi, i: (layer, 0, 0)),
                 pl.BlockSpec((None, d_ff, d), lambda bi, i: (layer, 0, 0))]
    return pl.pallas_call(
        functools.partial(_post_kernel, mode=mode, n_attn=len(attn), ff_chunk=1024),
        grid=(b, l // tm),
        in_specs=in_specs,
        out_specs=pl.BlockSpec((None, tm, d), tok),
        out_shape=jax.ShapeDtypeStruct((b, l, d), F32),
        compiler_params=_cparams("parallel", "parallel"),
        name="post_block",
    )(x, *attn, w_out, norm_g, *mods, w_up, w_down)


def _decode_window_kernel(*refs, n_pat, g, hkv, has_sink, has_gate):
    it = iter(refs)
    q_ref = next(it)
    new_ref = next(it)
    row_refs = [next(it) for _ in range(n_pat)]
    bias_ref = next(it)
    bias_new_ref = next(it)
    seg_ref = next(it)
    segt_ref = next(it)
    sink_ref = next(it) if has_sink else None
    gate_ref = next(it) if has_gate else None
    o_ref = next(it)
    wk = hkv * HEAD_DIM
    hp = lax.Precision.HIGHEST
    seg = seg_ref[...]
    segt = segt_ref[...]
    knew = new_ref[:, :wk]
    vnew = new_ref[:, wk:]
    for gi in range(g):
        qg = q_ref[gi:gi + 1, :] * SCALE
        s_new = jnp.dot(knew * qg, seg, precision=hp, preferred_element_type=F32)
        scores = []
        m = None
        for pi in range(n_pat):
            k = row_refs[pi][:, :wk]
            s = jnp.dot(k * qg, seg, precision=hp, preferred_element_type=F32) + bias_ref[pi, gi]
            sn = s_new + bias_new_ref[pi, gi]
            mp = jnp.maximum(jnp.max(s, axis=0, keepdims=True), sn)
            m = mp if m is None else jnp.maximum(m, mp)
            scores.append((s, sn))
        if has_sink:
            sk = sink_ref[gi:gi + 1, :]
            m = jnp.maximum(m, sk)
        l = jnp.zeros((1, LANES), F32)
        acc = jnp.zeros((1, wk), F32)
        for pi in range(n_pat):
            s, sn = scores[pi]
            p = jnp.exp(s - m)
            pn = jnp.exp(sn - m)
            l = l + jnp.sum(p, axis=0, keepdims=True) + pn
            pe = jnp.dot(p, segt, precision=hp, preferred_element_type=F32)
            pne = jnp.dot(pn, segt, precision=hp, preferred_element_type=F32)
            acc = acc + jnp.sum(pe * row_refs[pi][:, wk:], axis=0, keepdims=True) + pne * vnew
        if has_sink:
            l = l + jnp.exp(sk - m)
        scale = 1.0 / l
        if has_gate:
            scale = scale * jax.nn.sigmoid(gate_ref[gi:gi + 1, :])
        o_ref[gi:gi + 1, :] = acc * jnp.dot(scale, segt, precision=hp, preferred_element_type=F32)


def _decode_window(q, new, cache, layer, tab, patterns, *, hq, hkv, sink=None, gate=None):
    n = q.shape[0]
    g = hq // hkv
    wk = hkv * HEAD_DIM
    wb = cache.shape[2]
    qp = q.reshape(n, hkv, g, HEAD_DIM).transpose(0, 2, 1, 3).reshape(n, g, wk)
    in_specs = [pl.BlockSpec((None, g, wk), lambda i: (i, 0, 0)),
                pl.BlockSpec((None, 1, 2 * wk), lambda i: (i, 0, 0))]
    args = [qp, new.reshape(n, 1, 2 * wk)]
    lane_pad = ((0, 0), (0, 0), (0, LANES - hkv))
    biases, biases_new = [], []
    rows = patterns[0][0] // patterns[0][1]
    for window, dil in patterns:
        assert window <= wb and window // dil == rows and wb % (dil * rows) == 0
        view = cache.reshape(cache.shape[0], n, wb // dil, dil * 2 * wk)
        last = wb // dil // rows - 1
        in_specs.append(pl.BlockSpec((None, None, rows, 2 * wk), lambda i, last=last: (layer, i, last, 0)))
        args.append(view)
        dist = dil * (rows - jnp.arange(rows))
        bt = _bias_of(tab, dist).reshape(hkv, g, rows).transpose(1, 2, 0)
        biases.append(jnp.pad(bt, lane_pad))
        bn = _bias_of(tab, jnp.zeros((1,), jnp.int32)).reshape(hkv, g, 1).transpose(1, 2, 0)
        biases_new.append(jnp.pad(bn, lane_pad))
    bias = jnp.stack(biases)
    bias_new = jnp.stack(biases_new)
    seg = (jnp.arange(wk)[:, None] // HEAD_DIM == jnp.arange(LANES)[None, :]).astype(F32)
    consts = [bias, bias_new, seg, seg.T]
    if sink is not None:
        consts.append(jnp.pad(sink.reshape(hkv, g).T.astype(F32), lane_pad[1:]))
    for cst in consts:
        in_specs.append(pl.BlockSpec(cst.shape, lambda i, nd=cst.ndim: (0,) * nd))
        args.append(cst)
    if gate is not None:
        in_specs.append(pl.BlockSpec((None, g, LANES), lambda i: (i, 0, 0)))
        args.append(jnp.pad(gate.reshape(n, hkv, g).transpose(0, 2, 1), lane_pad))
    out = pl.pallas_call(
        functools.partial(_decode_window_kernel, n_pat=len(patterns), g=g, hkv=hkv,
                          has_sink=sink is not None, has_gate=gate is not None),
        grid=(n,),
        in_specs=in_specs,
        out_specs=pl.BlockSpec((None, g, wk), lambda i: (i, 0, 0)),
        out_shape=jax.ShapeDtypeStruct((n, g, wk), F32),
        compiler_params=_cparams("parallel"),
        name="decode_window",
    )(*args)
    return out.reshape(n, g, hkv, HEAD_DIM).transpose(0, 2, 1, 3).reshape(n, hq * HEAD_DIM)


def _paged_hab_kernel(pt_ref, *refs, pages):
    page_refs = refs[:pages]
    w_ref, o_ref = refs[pages:]
    z = jnp.concatenate([pr[...] for pr in page_refs], axis=0).astype(BF16)
    o_ref[...] = _dot(z, w_ref[...])


def _paged_hab(pool, page_table, w1big, layer, *, pages):
    lyr, n_pool, page, c = pool.shape
    n, n_pages = page_table.shape
    cpp = page // CMP_STRIDE
    kdim = CMP_STRIDE * c
    view = pool.reshape(lyr, n_pool, cpp, kdim)
    nout = w1big.shape[2]
    assert n_pages % pages == 0

    def page_map(k):
        return lambda i, j, pt: (layer, pt[i, j * pages + k], 0, 0)

    grid_spec = pltpu.PrefetchScalarGridSpec(
        num_scalar_prefetch=1,
        grid=(n, n_pages // pages),
        in_specs=[pl.BlockSpec((None, None, cpp, kdim), page_map(k)) for k in range(pages)]
        + [pl.BlockSpec((None, kdim, nout), lambda i, j, pt: (layer, 0, 0))],
        out_specs=pl.BlockSpec((None, pages * cpp, nout), lambda i, j, pt: (i, j, 0)),
    )
    return pl.pallas_call(
        functools.partial(_paged_hab_kernel, pages=pages),
        grid_spec=grid_spec,
        out_shape=jax.ShapeDtypeStruct((n, n_pages * cpp, nout), F32),
        compiler_params=_cparams("parallel", "parallel"),
        name="paged_hab",
    )(page_table, *([view] * pages), w1big)


def _cmp_sample_kernel(q_ref, ckv_ref, bias_ref, ov_ref, gate_ref, o_ref, idx_ref, *, nch, n_slc, qpos):
    g = NSA_HEADS // NSA_KV_HEADS
    wk = NSA_KV_HEADS * HEAD_DIM
    ckv = ckv_ref[...]
    c = lax.broadcasted_iota(jnp.int32, (g, nch), 1)
    mask = (c * CMP_STRIDE + CMP_LEN - 1 <= qpos) & (c < nch - 1)
    q = q_ref[...] * SCALE
    gates = jax.nn.sigmoid(gate_ref[...])
    n_slc_pad = ov_ref.shape[1]
    jb = lax.broadcasted_iota(jnp.int32, (1, n_slc_pad), 1)
    cur = qpos // SLC_BLOCK
    valid = (jb * SLC_BLOCK <= qpos) & (jb < n_slc)
    forced = (jb == 0) | (jb == cur) | (jb == cur - 1)
    k_sel = min(N_SELECT, n_slc)
    lane = lax.broadcasted_iota(jnp.int32, (1, LANES), 1)
    for kh in range(NSA_KV_HEADS):
        rs = slice(kh * g, (kh + 1) * g)
        k_h = ckv[:, kh * HEAD_DIM:(kh + 1) * HEAD_DIM]
        v_h = ckv[:, wk + kh * HEAD_DIM: wk + (kh + 1) * HEAD_DIM].astype(BF16)
        s = _dot_nt_3pass(q[rs, :], k_h) + bias_ref[rs, :]
        s = jnp.where(mask, s, NEG)
        m = jnp.max(s, axis=-1, keepdims=True)
        p = jnp.where(mask, jnp.exp(s - m), 0.0)
        p = p / jnp.maximum(jnp.sum(p, axis=-1, keepdims=True), 1e-30)
        o_ref[rs, :] = _dot(p.astype(BF16), v_h) * gates[rs, 0:1]
        imp = jnp.sum(_dot_2pass_lhs(p, ov_ref[...]), axis=0, keepdims=True)
        imp = jnp.where(valid, imp + jnp.where(forced, FORCE, 0.0), NEG)
        _, picks = _topk_select(imp, k_sel)
        row = jnp.zeros((1, LANES), jnp.int32)
        for t, idx in enumerate(picks):
            row = jnp.where(lane == t, idx.astype(jnp.int32), row)
        idx_ref[kh:kh + 1, :] = row


def _cmp_sample(q, ckv, bias, ov, gate, *, n_slc, qpos):
    n = q.shape[0]
    nch = ckv.shape[1]
    return pl.pallas_call(
        functools.partial(_cmp_sample_kernel, nch=nch, n_slc=n_slc, qpos=qpos),
        grid=(n,),
        in_specs=[pl.BlockSpec((None, NSA_HEADS, HEAD_DIM), lambda i: (i, 0, 0)),
                  pl.BlockSpec((None, nch, ckv.shape[2]), lambda i: (i, 0, 0)),
                  pl.BlockSpec(bias.shape, lambda i: (0, 0)),
                  pl.BlockSpec(ov.shape, lambda i: (0, 0)),
                  pl.BlockSpec((None, NSA_HEADS, 3), lambda i: (i, 0, 0))],
        out_specs=[pl.BlockSpec((None, NSA_HEADS, HEAD_DIM), lambda i: (i, 0, 0)),
                   pl.BlockSpec((None, NSA_KV_HEADS, LANES), lambda i: (i, 0, 0))],
        out_shape=[jax.ShapeDtypeStruct((n, NSA_HEADS, HEAD_DIM), F32),
                   jax.ShapeDtypeStruct((n, NSA_KV_HEADS, LANES), jnp.int32)],
        compiler_params=_cparams("parallel"),
        name="cmp_sample",
    )(q, ckv, bias, ov, gate)


def _slc_sample_kernel(blk_ref, idx_ref, q_ref, new_ref, kv_ref, bias_ref, gate_ref, o_ref,
                       m_ref, l_ref, acc_ref, *, nb_past, k_sel):
    i = pl.program_id(0)
    kvh = pl.program_id(1)
    t = pl.program_id(2)
    g = NSA_HEADS // NSA_KV_HEADS
    wk = NSA_KV_HEADS * HEAD_DIM
    q = q_ref[...] * SCALE

    def head_cols(x, base):
        return jnp.where(kvh == 0, x[:, base:base + HEAD_DIM], x[:, base + HEAD_DIM:base + 2 * HEAD_DIM])

    @pl.when(t == 0)
    def _():
        kn = head_cols(new_ref[...], 0)
        vn = head_cols(new_ref[...], wk)
        sn = jnp.sum(q * kn, axis=-1, keepdims=True) + bias_ref[nb_past, :, 0:1]
        m_ref[...] = sn
        l_ref[...] = jnp.ones((g, 1), F32)
        acc_ref[...] = jnp.broadcast_to(vn, (g, HEAD_DIM))

    idx = idx_ref[(i * NSA_KV_HEADS + kvh) * k_sel + t]

    @pl.when(idx < nb_past)
    def _():
        kv = kv_ref[...]
        k_h = head_cols(kv, 0).astype(BF16)
        v_h = head_cols(kv, wk).astype(BF16)
        s = _dot_nt(q.astype(BF16), k_h) + bias_ref[idx]
        m_prev = m_ref[...]
        m_new = jnp.maximum(m_prev, jnp.max(s, axis=-1, keepdims=True))
        alpha = jnp.exp(m_prev - m_new)
        p = jnp.exp(s - m_new)
        l_ref[...] = alpha * l_ref[...] + jnp.sum(p, axis=-1, keepdims=True)
        acc_ref[...] = alpha * acc_ref[...] + _dot(p.astype(BF16), v_h)
        m_ref[...] = m_new

    @pl.when(t == k_sel - 1)
    def _():
        o_ref[...] = acc_ref[...] / l_ref[...] * jax.nn.sigmoid(gate_ref[...])


def _slc_sample(q, new, pool, layer, blk_rows, idx, bias, gate, *, nb_past):
    n = q.shape[0]
    g = NSA_HEADS // NSA_KV_HEADS
    k_sel = blk_rows.shape[0] // (n * NSA_KV_HEADS)
    c = pool.shape[3]

    def sel_map(i, kh, t, blk, ids):
        return (layer, blk[(i * NSA_KV_HEADS + kh) * k_sel + t], 0, 0)

    grid_spec = pltpu.PrefetchScalarGridSpec(
        num_scalar_prefetch=2,
        grid=(n, NSA_KV_HEADS, k_sel),
        in_specs=[pl.BlockSpec((None, g, HEAD_DIM), lambda i, kh, t, blk, ids: (i, kh, 0)),
                  pl.BlockSpec((None, 1, c), lambda i, kh, t, blk, ids: (i, 0, 0)),
                  pl.BlockSpec((None, None, SLC_BLOCK, c), sel_map),
                  pl.BlockSpec((bias.shape[0], None, g, SLC_BLOCK), lambda i, kh, t, blk, ids: (0, kh, 0, 0)),
                  pl.BlockSpec((None, g, 1), lambda i, kh, t, blk, ids: (i, kh, 0))],
        out_specs=pl.BlockSpec((None, g, HEAD_DIM), lambda i, kh, t, blk, ids: (i, kh, 0)),
        scratch_shapes=[pltpu.VMEM((g, 1), F32), pltpu.VMEM((g, 1), F32), pltpu.VMEM((g, HEAD_DIM), F32)],
    )
    return pl.pallas_call(
        functools.partial(_slc_sample_kernel, nb_past=nb_past, k_sel=k_sel),
        grid_spec=grid_spec,
        out_shape=jax.ShapeDtypeStruct((n, NSA_HEADS, HEAD_DIM), F32),
        compiler_params=_cparams("parallel", "parallel", "arbitrary"),
        name="slc_sample",
    )(blk_rows, idx, q, new, pool, bias, gate)


EVEN_SPLITS = (A_HEADS * HEAD_DIM, 2 * A_KV_HEADS * HEAD_DIM, B_HEADS * HEAD_DIM, 2 * B_HEADS * HEAD_DIM)
NSA_KV_W = 2 * NSA_KV_HEADS * HEAD_DIM
NSA_SPLITS = (NSA_HEADS * HEAD_DIM, NSA_KV_W, NSA_KV_W, NSA_KV_W, LANES)


def _mod_parts(mod):
    if mod.ndim == 3:
        return [mod[:, i:i + 1] for i in range(6)]
    return [mod[:, :, i] for i in range(6)]


def _even_layer(xp, xs, mod_p, mod_s, cache_swa, cache_dil, li, layer, wts, tabs):
    w_in, w_out, sinks, norm_g, w_up, w_down = wts
    g0 = norm_g[layer, 0:1]
    sh1, sc1, gt1, sh2, sc2, gt2 = _mod_parts(mod_p)
    qa, kva, qb, kvb = _norm_linear(xp, g0, sc1, sh1, w_in, li, EVEN_SPLITS, tm=512)
    (oa,) = _banded_attn(qa, kva, tabs["swa"], hq=A_HEADS, hkv=A_KV_HEADS, window=WIN_A, sink=sinks[li])
    stats = []
    for (w, d), bias in zip(DIL_PAIRS, tabs["dil"]):
        stats += _banded_attn(qb, kvb, bias, hq=B_HEADS, hkv=B_HEADS, window=w // d, dil=d, mode="stats")
    xp_new = _post_block(xp, [oa] + stats, w_out, norm_g, (gt1, sc2, sh2, gt2), w_up, w_down, layer,
                         mode="even", tm=256)
    l = xp.shape[1]
    state_p = (kva[:, l - min(WIN_A, l):], kvb[:, l - min(DIL_MAX_WIN, l):])
    sh1, sc1, gt1, sh2, sc2, gt2 = _mod_parts(mod_s)
    n = xs.shape[1]
    qa, kva, qb, kvb = [a[0] for a in _norm_linear(xs, g0, sc1, sh1, w_in, li, EVEN_SPLITS, tm=n)]
    buf_a = cache_swa.reshape(cache_swa.shape[:3] + (-1,))
    buf_b = cache_dil.reshape(cache_dil.shape[:3] + (-1,))
    oa = _decode_window(qa, kva, buf_a, li, tabs["rel"][:, :A_HEADS], ((WIN_A, 1),),
                        hq=A_HEADS, hkv=A_KV_HEADS, sink=sinks[li])
    ob = _decode_window(qb, kvb, buf_b, li, tabs["rel"][:, A_HEADS:], DIL_PAIRS, hq=B_HEADS, hkv=B_HEADS)
    o = jnp.concatenate([oa, ob], axis=1)[None]
    xs_new = _post_block(xs, [o], w_out, norm_g, (gt1, sc2, sh2, gt2), w_up, w_down, layer,
                         mode="sum", tm=n)
    new_a = jnp.concatenate([buf_a[li, :, 1:], kva[:, None]], axis=1)
    new_b = jnp.concatenate([buf_b[li, :, 1:], kvb[:, None]], axis=1)
    return xp_new, xs_new, state_p, (new_a, new_b)


def _nsa_layer(xp, xs, mod_p, mod_s, pool_cmp, pool_slc, cache_win, page_table, li, layer, wts, tabs):
    w_in, w_out, w1big, w2big, pe_rows, norm_g, w_up, w_down = wts
    g0 = norm_g[layer, 0:1]
    b, l, _ = xp.shape
    pe_h = _linear(pe_rows[li], w1big, li, tm=8, tn=512)
    sh1, sc1, gt1, sh2, sc2, gt2 = _mod_parts(mod_p)
    q, kv_cmp, kv_slc, kv_win, gate = _norm_linear(xp, g0, sc1, sh1, w_in, li, NSA_SPLITS, tm=512)
    nch = l // CMP_STRIDE
    n_slc = -(-l // SLC_BLOCK)
    hab = _linear(kv_cmp.reshape(b * nch, CMP_STRIDE * NSA_KV_W), w1big, li, tm=min(512, b * nch), tn=512)
    pad = nch - Q_BLOCK // CMP_STRIDE
    ckv_pad = _compress_finish(hab.reshape(b, nch, -1), pe_h, w2big, li, pad=pad, rows_out=2 * nch)
    o_cmp, sel = _cmp_prompt(q, ckv_pad, tabs["cmp_rel"], tabs["ov_pad"], gate, nch=nch, n_slc=n_slc)
    o_slc = _slc_prompt(q, kv_slc, sel, tabs["slc_tiles"], gate)
    (o_win,) = _banded_attn(q, kv_win, tabs["win"], hq=NSA_HEADS, hkv=NSA_KV_HEADS, window=WIN_C,
                            gate=gate, gate_col=2)
    xp_new = _post_block(xp, [o_cmp, o_slc, o_win], w_out, norm_g, (gt1, sc2, sh2, gt2), w_up, w_down,
                         layer, mode="sum", tm=256)
    state_p = (kv_cmp, kv_slc, kv_win[:, l - min(WIN_C, l):])
    sh1, sc1, gt1, sh2, sc2, gt2 = _mod_parts(mod_s)
    n = xs.shape[1]
    q, kv_cmp_s, kv_slc_s, kv_win_s, gate = [a[0] for a in
                                             _norm_linear(xs, g0, sc1, sh1, w_in, li, NSA_SPLITS, tm=n)]
    n_pages = page_table.shape[1]
    page = pool_cmp.shape[2]
    past = n_pages * page
    assert past % SLC_BLOCK == 0 and past % CMP_STRIDE == 0
    nch_s = past // CMP_STRIDE
    nb_past = past // SLC_BLOCK
    n_slc_s = nb_past + 1
    pool_cmp_v = pool_cmp.reshape(pool_cmp.shape[:3] + (-1,))
    hab_s = _paged_hab(pool_cmp_v, page_table, w1big, li, pages=min(16, n_pages))
    ckv_s = _compress_finish(hab_s, pe_h, w2big, li, pad=0, rows_out=nch_s)
    gate3 = gate[:, :NSA_HEADS * 3].reshape(n, NSA_HEADS, 3)
    o_cmp, idx = _cmp_sample(q.reshape(n, NSA_HEADS, HEAD_DIM), ckv_s, tabs["cmp_s"], tabs["ov_s"], gate3,
                             n_slc=n_slc_s, qpos=past)
    k_sel = min(N_SELECT, n_slc_s)
    idx = idx[:, :, :k_sel]
    bpp = page // SLC_BLOCK
    pidx = jnp.clip(idx, 0, nb_past - 1)
    phys = jnp.take_along_axis(page_table, (pidx // bpp).reshape(n, -1), axis=1).reshape(idx.shape)
    blk_rows = (phys * bpp + pidx % bpp).reshape(-1)
    pool_slc_v = pool_slc.reshape(pool_slc.shape[0], pool_slc.shape[1] * bpp, SLC_BLOCK, -1)
    o_slc = _slc_sample(q.reshape(n, NSA_HEADS, HEAD_DIM), kv_slc_s[:, None], pool_slc_v, li, blk_rows,
                        idx.reshape(-1), tabs["slc_s"], gate3[:, :, 1:2], nb_past=nb_past)
    buf_w = cache_win.reshape(cache_win.shape[:3] + (-1,))
    o_win = _decode_window(q, kv_win_s, buf_w, li, tabs["rel"], ((WIN_C, 1),), hq=NSA_HEADS, hkv=NSA_KV_HEADS,
                           gate=gate3[:, :, 2])
    attn = [o_cmp.reshape(1, n, -1), o_slc.reshape(1, n, -1), o_win[None]]
    xs_new = _post_block(xs, attn, w_out, norm_g, (gt1, sc2, sh2, gt2), w_up, w_down, layer, mode="sum", tm=n)
    new_win = jnp.concatenate([buf_w[li, :, 1:], kv_win_s[:, None]], axis=1)
    return xp_new, xs_new, state_p, (kv_cmp_s, kv_slc_s, new_win)


def kernel(x_prompt, x_sample, cache_swa_kv, cache_dil_kv, cache_nsa_cmp, cache_nsa_slc, cache_nsa_win,
           page_table, c_prompt, c_sample, rel_bias, ada_w, ada_b, norm_g, mlp_up, mlp_down,
           even_w_in, even_w_out, attn_sinks, nsa_w_in, nsa_w_out, cmp_w1, cmp_w2, cmp_pe):
    depth, d, _ = ada_w.shape
    b, l, _ = x_prompt.shape
    n = x_sample.shape[0]
    assert x_sample.shape[1] == 1
    assert cache_swa_kv.shape[2] == WIN_A and cache_dil_kv.shape[2] == DIL_MAX_WIN and cache_nsa_win.shape[2] == WIN_C

    ada_wb = ada_w.astype(BF16)
    up_b = mlp_up.astype(BF16)
    down_b = mlp_down.astype(BF16)
    even_in_b = even_w_in.astype(BF16)
    even_out_b = even_w_out.astype(BF16)
    nsa_in_b = jnp.pad(nsa_w_in, ((0, 0), (0, 0), (0, sum(NSA_SPLITS) - nsa_w_in.shape[2]))).astype(BF16)
    nsa_out_b = nsa_w_out.astype(BF16)
    w1big, w2big = _compress_weights(cmp_w1, cmp_w2)
    pe_rows = _pe_rows(cmp_pe)

    c_all = jnp.concatenate([c_prompt, c_sample], axis=0)
    mods = [_linear(c_all, ada_wb, layer, ada_b[:, None, :], pre="silu", tm=b + n, tn=1536)
            for layer in range(depth)]

    nq = l // Q_BLOCK
    nch = l // CMP_STRIDE
    n_slc_pad = _round_up(-(-l // SLC_BLOCK), LANES)
    past = page_table.shape[1] * cache_nsa_cmp.shape[2]
    nch_s = past // CMP_STRIDE
    nb_past = past // SLC_BLOCK
    w_rel = jnp.arange(nch) - (nch - Q_BLOCK // CMP_STRIDE)
    cmp_rel_dist = jnp.arange(Q_BLOCK)[:, None] - CMP_STRIDE * w_rel[None, :] - (CMP_LEN - 1)
    ov = _overlap_table(nch, n_slc_pad)
    pad = nch - Q_BLOCK // CMP_STRIDE
    g = NSA_HEADS // NSA_KV_HEADS
    slc_s_dist = past - (jnp.arange(nb_past + 1)[:, None] * SLC_BLOCK + jnp.arange(SLC_BLOCK)[None, :])
    tabs = {
        "rel": rel_bias,
        "swa": _banded_bias(rel_bias[:, :A_HEADS], WIN_A, 1),
        "dil": [_banded_bias(rel_bias[:, A_HEADS:], w // dd, dd) for w, dd in DIL_PAIRS],
        "win": _banded_bias(rel_bias, WIN_C, 1),
        "cmp_rel": _bias_of(rel_bias, cmp_rel_dist),
        "ov_pad": jnp.pad(ov, ((pad, 2 * nch - pad - nch), (0, 0))),
        "slc_tiles": _slc_bias_tiles(rel_bias, nq),
        "cmp_s": _bias_of(rel_bias, past - (jnp.arange(nch_s) * CMP_STRIDE + CMP_LEN - 1)),
        "ov_s": _overlap_table(nch_s, _round_up(nb_past + 1, LANES)),
        "slc_s": _bias_of(rel_bias, slc_s_dist).reshape(NSA_KV_HEADS, g, nb_past + 1, SLC_BLOCK).transpose(2, 0, 1, 3),
    }

    xp = x_prompt
    xs = x_sample.reshape(1, n, d)
    outs = {k: [] for k in ("swa_p", "swa_s", "dil_p", "dil_s", "cmp_p", "cmp_s", "slc_p", "slc_s", "win_p", "win_s")}
    for layer in range(depth):
        li = layer // 2
        mod = mods[layer].reshape(b + n, 6, d)
        mod_p, mod_s = mod[:b], mod[b:][None]
        if layer % 2 == 0:
            wts = (even_in_b, even_out_b, attn_sinks, norm_g, up_b, down_b)
            xp, xs, (sa, sb), (ta, tb) = _even_layer(xp, xs, mod_p, mod_s, cache_swa_kv, cache_dil_kv, li,
                                                     layer, wts, tabs)
            outs["swa_p"].append(sa.reshape(b, -1, 2, A_KV_HEADS, HEAD_DIM))
            outs["dil_p"].append(sb.reshape(b, -1, 2, B_HEADS, HEAD_DIM))
            outs["swa_s"].append(ta.reshape(n, -1, 2, A_KV_HEADS, HEAD_DIM))
            outs["dil_s"].append(tb.reshape(n, -1, 2, B_HEADS, HEAD_DIM))
        else:
            wts = (nsa_in_b, nsa_out_b, w1big, w2big, pe_rows, norm_g, up_b, down_b)
            xp, xs, (pc, ps, pw), (qc, qs_, qw) = _nsa_layer(xp, xs, mod_p, mod_s, cache_nsa_cmp, cache_nsa_slc,
                                                             cache_nsa_win, page_table, li, layer, wts, tabs)
            kvshape = (2, NSA_KV_HEADS, HEAD_DIM)
            outs["cmp_p"].append(pc.reshape(b, -1, *kvshape))
            outs["slc_p"].append(ps.reshape(b, -1, *kvshape))
            outs["win_p"].append(pw.reshape(b, -1, *kvshape))
            outs["cmp_s"].append(qc.reshape(n, 1, *kvshape))
            outs["slc_s"].append(qs_.reshape(n, 1, *kvshape))
            outs["win_s"].append(qw.reshape(n, -1, *kvshape))
    st = {k: jnp.stack(v) for k, v in outs.items()}
    return (xp, xs.reshape(n, 1, d), st["swa_p"], st["swa_s"], st["dil_p"], st["dil_s"],
            st["cmp_p"], st["cmp_s"], st["slc_p"], st["slc_s"], st["win_p"], st["win_s"])
```

```python
import functools
import math

import jax
import jax.numpy as jnp
from jax import lax
from jax.experimental import pallas as pl
from jax.experimental.pallas import tpu as pltpu

F32 = jnp.float32
BF16 = jnp.bfloat16

HEAD_DIM = 64
A_HEADS = 8
A_KV_HEADS = 2
B_HEADS = 8
WIN_A = 128
DIL_PAIRS = ((128, 1), (512, 4), (2048, 16))
DIL_MAX_WIN = 2048
NSA_HEADS = 16
NSA_KV_HEADS = 2
CMP_STRIDE = 16
CMP_LEN = 32
CMP_HID = 128
SLC_BLOCK = 64
N_SELECT = 16
WIN_C = 512
RP_BUCKETS = 32
RP_MAX_DIST = 2048
Q_BLOCK = 128
EPS = 1e-6
NEG = -1e30
REMOVED = -3e38
FORCE = 1e6
SCALE = HEAD_DIM ** -0.5
SLC_TILES_PER_STEP = 4

VMEM_LIMIT_BYTES = 56 * 1024 * 1024
LANES = 128


def _cparams(*sem):
    return pltpu.CompilerParams(dimension_semantics=sem, vmem_limit_bytes=VMEM_LIMIT_BYTES)


def _round_up(x, m):
    return (x + m - 1) // m * m


def _t5_bucket(dist):
    n = dist.astype(jnp.int32)
    exact = RP_BUCKETS // 2
    val = jnp.log(jnp.maximum(n, 1).astype(F32) / exact) / math.log(RP_MAX_DIST / exact)
    large = jnp.minimum(exact + (val * (RP_BUCKETS - exact)).astype(jnp.int32), RP_BUCKETS - 1)
    return jnp.where(n < exact, n, large)


def _bias_of(tab, dist):
    b = tab[_t5_bucket(jnp.maximum(dist, 0))].astype(F32)
    return jnp.moveaxis(b, -1, 0)


def _rms(x, g):
    return x * lax.rsqrt(jnp.mean(x * x, axis=-1, keepdims=True) + EPS) * g


def _dot(a, b):
    return jnp.dot(a, b, preferred_element_type=F32)


def _dot_nt(a, b):
    return lax.dot_general(a, b, (((1,), (1,)), ((), ())), preferred_element_type=F32)


def _split_bf16(x):
    hi = x.astype(BF16)
    lo = (x - hi.astype(F32)).astype(BF16)
    return hi, lo


def _dot_nt_3pass(a, b):
    ah, al = _split_bf16(a)
    bh, bl = _split_bf16(b)
    return _dot_nt(ah, bh) + (_dot_nt(ah, bl) + _dot_nt(al, bh))


def _dot_2pass_lhs(a, b_bf16):
    ah, al = _split_bf16(a)
    return _dot(ah, b_bf16) + _dot(al, b_bf16)


def _gelu_tanh(x):
    return 0.5 * x * (1.0 + jnp.tanh(math.sqrt(2.0 / math.pi) * (x + 0.044715 * (x * x * x))))


def _linear_kernel(*refs, pre, has_bias):
    if has_bias:
        a_ref, b_ref, bias_ref, o_ref = refs
    else:
        a_ref, b_ref, o_ref = refs
    a = a_ref[...]
    if pre == "silu":
        a = a * jax.nn.sigmoid(a)
    acc = _dot(a.astype(BF16), b_ref[...])
    if has_bias:
        acc = acc + bias_ref[...]
    o_ref[...] = acc


def _linear(a, b, layer, bias=None, *, pre=None, tm, tn):
    m, k = a.shape
    n = b.shape[2]
    assert m % tm == 0 and n % tn == 0
    in_specs = [pl.BlockSpec((tm, k), lambda i, j: (i, 0)),
                pl.BlockSpec((None, k, tn), lambda i, j: (layer, 0, j))]
    args = [a, b]
    if bias is not None:
        in_specs.append(pl.BlockSpec((None, 1, tn), lambda i, j: (layer, 0, j)))
        args.append(bias)
    return pl.pallas_call(
        functools.partial(_linear_kernel, pre=pre, has_bias=bias is not None),
        grid=(m // tm, n // tn),
        in_specs=in_specs,
        out_specs=pl.BlockSpec((tm, tn), lambda i, j: (i, j)),
        out_shape=jax.ShapeDtypeStruct((m, n), F32),
        compiler_params=_cparams("parallel", "parallel"),
        name="linear",
    )(*args)


def _norm_linear_kernel(*refs, splits, splits_t):
    x_ref, g_ref, sc_ref, sh_ref, w_ref = refs[:5]
    rest = refs[5:]
    if splits_t:
        wt_ref, rest = rest[0], rest[1:]
    o_refs = rest[:len(splits)]
    ot_refs = rest[len(splits):]
    h = _rms(x_ref[...], g_ref[...]) * (1.0 + sc_ref[...]) + sh_ref[...]
    hb = h.astype(BF16)
    off = 0
    for o_ref, sz in zip(o_refs, splits):
        o_ref[...] = _dot(hb, w_ref[:, off:off + sz])
        off += sz
    off = 0
    for o_ref, sz in zip(ot_refs, splits_t):
        o_ref[...] = _dot_nt(wt_ref[off:off + sz, :], hb)
        off += sz


def _norm_linear(x, g, sc, sh, w, layer, splits, *, tm, w_t=None, splits_t=()):
    b, l, d = x.shape
    n = w.shape[2]
    assert sum(splits) == n and l % tm == 0
    r = sc.shape[1]
    rb = 1 if r == 1 else tm
    mod_map = (lambda bi, i: (bi, 0, 0)) if r == 1 else (lambda bi, i: (bi, i, 0))
    in_specs = [pl.BlockSpec((None, tm, d), lambda bi, i: (bi, i, 0)),
                pl.BlockSpec((1, d), lambda bi, i: (0, 0)),
                pl.BlockSpec((None, rb, d), mod_map),
                pl.BlockSpec((None, rb, d), mod_map),
                pl.BlockSpec((None, d, n), lambda bi, i: (layer, 0, 0))]
    args = [x, g, sc, sh, w]
    if splits_t:
        assert sum(splits_t) == w_t.shape[1]
        in_specs.append(pl.BlockSpec((None, w_t.shape[1], d), lambda bi, i: (layer, 0, 0)))
        args.append(w_t)
    return pl.pallas_call(
        functools.partial(_norm_linear_kernel, splits=splits, splits_t=splits_t),
        grid=(b, l // tm),
        in_specs=in_specs,
        out_specs=[pl.BlockSpec((None, tm, sz), lambda bi, i: (bi, i, 0)) for sz in splits]
        + [pl.BlockSpec((None, sz, tm), lambda bi, i: (bi, 0, i)) for sz in splits_t],
        out_shape=[jax.ShapeDtypeStruct((b, l, sz), F32) for sz in splits]
        + [jax.ShapeDtypeStruct((b, sz, l), F32) for sz in splits_t],
        compiler_params=_cparams("parallel", "parallel"),
        name="norm_linear",
    )(*args)


def _banded_kernel(*refs, hq, hkv, nb, window, mode, has_sink, gate_col):
    it = iter(refs)
    q_ref = next(it)
    k_refs = [next(it) for _ in range(nb + 1)]
    v_refs = [next(it) for _ in range(nb + 1)]
    bias_ref = next(it)
    sink_ref = next(it) if has_sink else None
    gate_ref = next(it) if gate_col is not None else None
    outs = list(it)
    g = hq // hkv
    kw = (nb + 1) * Q_BLOCK
    i = pl.program_id(2)

    r = lax.broadcasted_iota(jnp.int32, (Q_BLOCK, kw), 0)
    c = lax.broadcasted_iota(jnp.int32, (Q_BLOCK, kw), 1)
    dist = r + nb * Q_BLOCK - c
    kpos = (i - nb) * Q_BLOCK + c
    mask = (dist >= 0) & (dist <= window) & (kpos >= 0)

    q = q_ref[...] * SCALE
    kcat = jnp.concatenate([kr[...] for kr in k_refs], axis=0).astype(BF16)
    vcat = jnp.concatenate([vr[...] for vr in v_refs], axis=0).astype(BF16)
    if gate_ref is not None:
        gates = jax.nn.sigmoid(gate_ref[...])
    for h in range(hq):
        kh = h // g
        hs = slice(h * HEAD_DIM, (h + 1) * HEAD_DIM)
        ks = slice(kh * HEAD_DIM, (kh + 1) * HEAD_DIM)
        s = _dot_nt(q[:, hs].astype(BF16), kcat[:, ks]) + bias_ref[h]
        s = jnp.where(mask, s, NEG)
        m = jnp.max(s, axis=-1, keepdims=True)
        p = jnp.exp(s - m)
        l = jnp.sum(p, axis=-1, keepdims=True)
        acc = _dot(p.astype(BF16), vcat[:, ks])
        if mode == "stats":
            acc_ref, m_ref, l_ref = outs
            acc_ref[:, hs] = acc
            m_ref[:, hs] = jnp.broadcast_to(m, (Q_BLOCK, HEAD_DIM))
            l_ref[:, hs] = jnp.broadcast_to(l, (Q_BLOCK, HEAD_DIM))
        else:
            (o_ref,) = outs
            if has_sink:
                sk = sink_ref[h]
                big = jnp.maximum(m, sk)
                w = jnp.exp(m - big)
                o = (acc * w) / (l * w + jnp.exp(sk - big))
            else:
                o = acc / l
            if gate_ref is not None:
                col = h * 3 + gate_col
                o = o * gates[:, col:col + 1]
            o_ref[:, hs] = o


def _banded_attn(q, kv, bias, *, hq, hkv, window, dil=1, mode="norm", sink=None,
                 gate=None, gate_col=None):
    b, l, wq = q.shape
    wk = hkv * HEAD_DIM
    assert wq == hq * HEAD_DIM and kv.shape == (b, l, 2 * wk)
    ls = l // dil
    assert l % dil == 0 and ls % Q_BLOCK == 0
    nq = ls // Q_BLOCK
    nb = -(-window // Q_BLOCK)
    qv = q.reshape(b, ls, dil * wq)
    kvv = kv.reshape(b, ls, dil * 2 * wk)

    def kmap(j, half):
        return lambda bi, r, i: (bi, jnp.maximum(i - nb + j, 0), 2 * r + half)

    in_specs = [pl.BlockSpec((None, Q_BLOCK, wq), lambda bi, r, i: (bi, i, r))]
    args = [qv]
    for half in (0, 1):
        for j in range(nb + 1):
            in_specs.append(pl.BlockSpec((None, Q_BLOCK, wk), kmap(j, half)))
            args.append(kvv)
    in_specs.append(pl.BlockSpec(bias.shape, lambda bi, r, i: (0, 0, 0)))
    args.append(bias)
    if sink is not None:
        in_specs.append(pl.BlockSpec(memory_space=pltpu.SMEM))
        args.append(sink)
    if gate is not None:
        assert dil == 1
        in_specs.append(pl.BlockSpec((None, Q_BLOCK, gate.shape[2]), lambda bi, r, i: (bi, i, 0)))
        args.append(gate)
    n_out = 3 if mode == "stats" else 1
    out = pl.pallas_call(
        functools.partial(_banded_kernel, hq=hq, hkv=hkv, nb=nb, window=window, mode=mode,
                          has_sink=sink is not None, gate_col=gate_col if gate is not None else None),
        grid=(b, dil, nq),
        in_specs=in_specs,
        out_specs=[pl.BlockSpec((None, Q_BLOCK, wq), lambda bi, r, i: (bi, i, r))] * n_out,
        out_shape=[jax.ShapeDtypeStruct((b, ls, dil * wq), F32)] * n_out,
        compiler_params=_cparams("parallel", "parallel", "parallel"),
        name="banded_attn",
    )(*args)
    return [o.reshape(b, l, wq) for o in out]


def _banded_bias(tab, window, dist_scale):
    nb = -(-window // Q_BLOCK)
    kw = (nb + 1) * Q_BLOCK
    r = jnp.arange(Q_BLOCK)[:, None]
    c = jnp.arange(kw)[None, :]
    return _bias_of(tab, jnp.maximum(r + nb * Q_BLOCK - c, 0) * dist_scale)


def _compress_finish_kernel(hab_ref, pe_ref, w2_ref, o_ref, *, nch, pad):
    hid_w = hab_ref.shape[1] // 2
    hab = hab_ref[...]
    ha = hab[:, :hid_w]
    hb_next = pltpu.roll(hab[:, hid_w:], nch - 1, axis=0)
    pe = pe_ref[0:1, :hid_w] + pe_ref[1:2, hid_w:]
    hid = _gelu_tanh(ha + hb_next + pe)
    ckv = _dot(hid.astype(BF16), w2_ref[...])
    if pad:
        o_ref[0:pad, :] = jnp.zeros((pad, o_ref.shape[1]), F32)
        o_ref[pad + nch:, :] = jnp.zeros((o_ref.shape[0] - pad - nch, o_ref.shape[1]), F32)
    o_ref[pad:pad + nch, :] = ckv


def _compress_finish(hab, pe_rows, w2big, layer, *, pad, rows_out):
    b, nch, w = hab.shape
    n_out = w2big.shape[2]
    return pl.pallas_call(
        functools.partial(_compress_finish_kernel, nch=nch, pad=pad),
        grid=(b,),
        in_specs=[pl.BlockSpec((None, nch, w), lambda bi: (bi, 0, 0)),
                  pl.BlockSpec(pe_rows.shape, lambda bi: (0, 0)),
                  pl.BlockSpec((None,) + w2big.shape[1:], lambda bi: (layer, 0, 0))],
        out_specs=pl.BlockSpec((None, rows_out, n_out), lambda bi: (bi, 0, 0)),
        out_shape=jax.ShapeDtypeStruct((b, rows_out, n_out), F32),
        compiler_params=_cparams("parallel"),
        name="compress_finish",
    )(hab, pe_rows, w2big)


def _compress_weights(cmp_w1, cmp_w2):
    lyr = cmp_w1.shape[0]
    eye_h = jnp.eye(NSA_KV_HEADS, dtype=F32)
    eye_p = jnp.eye(2, dtype=F32)
    w1 = cmp_w1.reshape(lyr, 2, 2, CMP_STRIDE, HEAD_DIM, CMP_HID)
    w1big = jnp.einsum("lpasde,pq,hk->lsphdaqke", w1, eye_p, eye_h)
    w1big = w1big.reshape(lyr, CMP_STRIDE * 2 * NSA_KV_HEADS * HEAD_DIM, 2 * 2 * NSA_KV_HEADS * CMP_HID)
    w2big = jnp.einsum("lped,pq,hk->lpheqkd", cmp_w2, eye_p, eye_h)
    w2big = w2big.reshape(lyr, 2 * NSA_KV_HEADS * CMP_HID, 2 * NSA_KV_HEADS * HEAD_DIM)
    w1p = jnp.einsum("lpasde,hk->lsphdake", w1, eye_h)
    w1p = w1p.reshape(lyr, CMP_STRIDE, 2, NSA_KV_HEADS * HEAD_DIM, 2 * NSA_KV_HEADS * CMP_HID)
    return w1big.astype(BF16), w2big.astype(BF16), w1p.astype(BF16)


def _pe_rows(cmp_pe):
    lyr = cmp_pe.shape[0]
    pe = cmp_pe.reshape(lyr, 2, 2, CMP_STRIDE, HEAD_DIM)
    pe = jnp.broadcast_to(pe[:, :, :, :, None, :], (lyr, 2, 2, CMP_STRIDE, NSA_KV_HEADS, HEAD_DIM))
    pe = pe.transpose(0, 2, 3, 1, 4, 5).reshape(lyr, 2, CMP_STRIDE * 2 * NSA_KV_HEADS * HEAD_DIM)
    return jnp.pad(pe, ((0, 0), (0, 6), (0, 0)))


def _overlap_table(nch, n_slc_pad):
    c = jnp.arange(nch)[:, None]
    jb = jnp.arange(n_slc_pad)[None, :]
    cstart = c * CMP_STRIDE
    cend = cstart + CMP_LEN - 1
    ov = (cstart <= jb * SLC_BLOCK + SLC_BLOCK - 1) & (cend >= jb * SLC_BLOCK) & (c < nch - 1)
    return ov.astype(BF16)


def _topk_select(imp, k_sel):
    lane = lax.broadcasted_iota(jnp.int32, imp.shape, 1).astype(F32)
    sel = jnp.zeros(imp.shape, F32)
    work = imp
    picks = []
    for _ in range(k_sel):
        mx = jnp.max(work, axis=-1, keepdims=True)
        idx = jnp.min(jnp.where(work == mx, lane, float(imp.shape[1])), axis=-1, keepdims=True)
        pick = lane == idx
        sel = jnp.where(pick, 1.0, sel)
        work = jnp.where(pick, REMOVED, work)
        picks.append(idx)
    return sel, picks


def _cmp_prompt_kernel(q_ref, ckv_ref, bias_ref, ov_ref, gate_ref, o_ref, sel_ref, *, nch, n_slc):
    i = pl.program_id(1)
    g = NSA_HEADS // NSA_KV_HEADS
    wk = NSA_KV_HEADS * HEAD_DIM
    start = pl.multiple_of(i * (Q_BLOCK // CMP_STRIDE), 8)
    win = ckv_ref[pl.ds(start, nch), :]
    ov = ov_ref[pl.ds(start, nch), :]
    r = lax.broadcasted_iota(jnp.int32, (Q_BLOCK, nch), 0)
    w = lax.broadcasted_iota(jnp.int32, (Q_BLOCK, nch), 1)
    e = w - (nch - Q_BLOCK // CMP_STRIDE)
    dist = r - CMP_STRIDE * e - (CMP_LEN - 1)
    cidx = e + i * (Q_BLOCK // CMP_STRIDE)
    mask = (dist >= 0) & (cidx >= 0)
    gates = jax.nn.sigmoid(gate_ref[...])
    q = q_ref[...] * SCALE

    n_slc_pad = ov_ref.shape[1]
    jb = lax.broadcasted_iota(jnp.int32, (Q_BLOCK, n_slc_pad), 1)
    qpos = i * Q_BLOCK + lax.broadcasted_iota(jnp.int32, (Q_BLOCK, n_slc_pad), 0)
    cur = qpos // SLC_BLOCK
    valid = (jb * SLC_BLOCK <= qpos) & (jb < n_slc)
    forced = (jb == 0) | (jb == cur) | (jb == cur - 1)

    for kh in range(NSA_KV_HEADS):
        k_h = win[:, kh * HEAD_DIM:(kh + 1) * HEAD_DIM]
        v_h = win[:, wk + kh * HEAD_DIM: wk + (kh + 1) * HEAD_DIM].astype(BF16)
        psum = jnp.zeros((Q_BLOCK, nch), F32)
        for gi in range(g):
            h = kh * g + gi
            hs = slice(h * HEAD_DIM, (h + 1) * HEAD_DIM)
            s = _dot_nt_3pass(q[:, hs], k_h) + bias_ref[h]
            s = jnp.where(mask, s, NEG)
            m = jnp.max(s, axis=-1, keepdims=True)
            p = jnp.where(mask, jnp.exp(s - m), 0.0)
            p = p / jnp.maximum(jnp.sum(p, axis=-1, keepdims=True), 1e-30)
            o = _dot(p.astype(BF16), v_h)
            o_ref[:, hs] = o * gates[:, h * 3:h * 3 + 1]
            psum = psum + p
        imp = _dot_2pass_lhs(psum, ov)
        imp = jnp.where(valid, imp + jnp.where(forced, FORCE, 0.0), NEG)
        sel, _ = _topk_select(imp, min(N_SELECT, n_slc))
        sel_ref[kh] = sel.astype(BF16)


def _cmp_prompt(q, ckv_pad, bias_rel, ov_pad, gate, *, nch, n_slc):
    b, l, wq = q.shape
    nq = l // Q_BLOCK
    n_slc_pad = ov_pad.shape[1]
    return pl.pallas_call(
        functools.partial(_cmp_prompt_kernel, nch=nch, n_slc=n_slc),
        grid=(b, nq),
        in_specs=[pl.BlockSpec((None, Q_BLOCK, wq), lambda bi, i: (bi, i, 0)),
                  pl.BlockSpec((None,) + ckv_pad.shape[1:], lambda bi, i: (bi, 0, 0)),
                  pl.BlockSpec(bias_rel.shape, lambda bi, i: (0, 0, 0)),
                  pl.BlockSpec(ov_pad.shape, lambda bi, i: (0, 0)),
                  pl.BlockSpec((None, Q_BLOCK, gate.shape[2]), lambda bi, i: (bi, i, 0))],
        out_specs=[pl.BlockSpec((None, Q_BLOCK, wq), lambda bi, i: (bi, i, 0)),
                   pl.BlockSpec((None, NSA_KV_HEADS, Q_BLOCK, n_slc_pad), lambda bi, i: (bi, 0, i, 0))],
        out_shape=[jax.ShapeDtypeStruct((b, l, wq), F32),
                   jax.ShapeDtypeStruct((b, NSA_KV_HEADS, l, n_slc_pad), BF16)],
        compiler_params=_cparams("parallel", "parallel"),
        name="cmp_prompt",
    )(q, ckv_pad, bias_rel, ov_pad, gate)


def _slc_prompt_kernel(q_ref, k_ref, vt_ref, sel_ref, bias_ref, gate_ref, o_ref,
                       qt_ref, m_ref, l_ref, acc_ref, *, nd):
    kvh = pl.program_id(1)
    i = pl.program_id(2)
    g = NSA_HEADS // NSA_KV_HEADS
    ql = g * Q_BLOCK
    pairs = g // 2
    q = q_ref[...] * SCALE
    qt_ref[...] = jnp.zeros(qt_ref.shape, BF16)
    row0 = pl.multiple_of(kvh * HEAD_DIM, HEAD_DIM)
    for pr in range(pairs):
        t = q[:, pr * LANES:(pr + 1) * LANES].T
        for half in range(2):
            h = 2 * pr + half
            qt_ref[pl.ds(row0, HEAD_DIM), h * Q_BLOCK:(h + 1) * Q_BLOCK] = (
                t[half * HEAD_DIM:(half + 1) * HEAD_DIM, :].astype(BF16))
    m_ref[...] = jnp.full(m_ref.shape, NEG, F32)
    l_ref[...] = jnp.zeros(l_ref.shape, F32)
    acc_ref[...] = jnp.zeros(acc_ref.shape, F32)

    sel = sel_ref[...]
    n_slc_pad = sel.shape[1]
    key_blk = lax.broadcasted_iota(jnp.int32, (Q_BLOCK, n_slc_pad), 0) // SLC_BLOCK
    blk_id = lax.broadcasted_iota(jnp.int32, (Q_BLOCK, n_slc_pad), 1)
    kk = lax.broadcasted_iota(jnp.int32, (Q_BLOCK, Q_BLOCK), 0)
    qq = lax.broadcasted_iota(jnp.int32, (Q_BLOCK, Q_BLOCK), 1)
    bpt = Q_BLOCK // SLC_BLOCK

    def body(jj, carry):
        k_parts, v_parts, b_parts, m_parts = [], [], [], []
        for sub in range(SLC_TILES_PER_STEP):
            j = jj * SLC_TILES_PER_STEP + sub
            jc = jnp.minimum(j, i)
            col0 = pl.multiple_of(jc * Q_BLOCK, Q_BLOCK)
            k_parts.append(k_ref[pl.ds(col0, Q_BLOCK), :].astype(BF16))
            v_parts.append(vt_ref[pl.ds(row0, HEAD_DIM), pl.ds(col0, Q_BLOCK)].astype(BF16))
            b_parts.append(bias_ref[jnp.minimum(i - jc, nd)])
            expand = jnp.where(blk_id == jc * bpt + key_blk, 1.0, 0.0).astype(BF16)
            picked = _dot_nt(expand, sel) > 0.5
            m_parts.append(jnp.where(picked & ((kk <= qq) | (j < i)) & (j <= i), 1.0, 0.0))
        keep = jnp.concatenate(m_parts, axis=0)
        drop = (keep - 1.0) * (-NEG)
        st = _dot(jnp.concatenate(k_parts, axis=0), qt_ref[...]) + jnp.concatenate(b_parts, axis=0)
        vt = jnp.concatenate(v_parts, axis=1)
        for h in range(g):
            hs = slice(h * Q_BLOCK, (h + 1) * Q_BLOCK)
            s_h = st[:, hs] + drop
            m_prev = m_ref[:, hs]
            m_new = jnp.maximum(m_prev, jnp.max(s_h, axis=0, keepdims=True))
            alpha = jnp.exp(m_prev - m_new)
            p = jnp.exp(s_h - m_new)
            l_ref[:, hs] = alpha * l_ref[:, hs] + jnp.sum(p, axis=0, keepdims=True)
            acc_ref[:, hs] = alpha * acc_ref[:, hs] + _dot(vt, p.astype(BF16))
            m_ref[:, hs] = m_new
        return carry

    lax.fori_loop(0, (i + SLC_TILES_PER_STEP) // SLC_TILES_PER_STEP, body, 0)
    ot = acc_ref[...] / l_ref[...]
    grow0 = pl.multiple_of(NSA_HEADS + kvh * g, g)
    gates = jax.nn.sigmoid(gate_ref[pl.ds(grow0, g), :])
    for pr in range(pairs):
        parts = []
        for half in range(2):
            h = 2 * pr + half
            parts.append(ot[:, h * Q_BLOCK:(h + 1) * Q_BLOCK] * gates[h:h + 1, :])
        o_ref[:, pr * LANES:(pr + 1) * LANES] = jnp.concatenate(parts, axis=0).T


def _slc_prompt(q, kv_slc, vt_slc, sel, bias_tiles, gate_t):
    b, l, wq = q.shape
    nq = l // Q_BLOCK
    g = NSA_HEADS // NSA_KV_HEADS
    wg = g * HEAD_DIM
    wk = NSA_KV_HEADS * HEAD_DIM
    nd = bias_tiles.shape[0] - 1
    n_slc_pad = sel.shape[3]
    ql = g * Q_BLOCK
    return pl.pallas_call(
        functools.partial(_slc_prompt_kernel, nd=nd),
        grid=(b, NSA_KV_HEADS, nq),
        in_specs=[pl.BlockSpec((None, Q_BLOCK, wg), lambda bi, kh, i: (bi, i, kh)),
                  pl.BlockSpec((None, l, wk), lambda bi, kh, i: (bi, 0, 0)),
                  pl.BlockSpec((None, wk, l), lambda bi, kh, i: (bi, 0, 0)),
                  pl.BlockSpec((None, None, Q_BLOCK, n_slc_pad), lambda bi, kh, i: (bi, kh, i, 0)),
                  pl.BlockSpec((nd + 1, None, Q_BLOCK, ql), lambda bi, kh, i: (0, kh, 0, 0)),
                  pl.BlockSpec((None, gate_t.shape[1], Q_BLOCK), lambda bi, kh, i: (bi, 0, i))],
        out_specs=pl.BlockSpec((None, Q_BLOCK, wg), lambda bi, kh, i: (bi, i, kh)),
        out_shape=jax.ShapeDtypeStruct((b, l, wq), F32),
        scratch_shapes=[pltpu.VMEM((wk, ql), BF16), pltpu.VMEM((1, ql), F32), pltpu.VMEM((1, ql), F32),
                        pltpu.VMEM((HEAD_DIM, ql), F32)],
        compiler_params=_cparams("parallel", "parallel", "arbitrary"),
        name="slc_prompt",
    )(q, kv_slc, vt_slc, sel, bias_tiles, gate_t)


def _slc_bias_tiles(tab, nq):
    nd = min(nq, 13)
    heads = tab.shape[1]
    g = NSA_HEADS // NSA_KV_HEADS
    t = _bias_of(tab, jnp.arange((nd + 1) * Q_BLOCK))
    u = jnp.arange(2 * Q_BLOCK)
    rel = (u + Q_BLOCK) % (2 * Q_BLOCK) - Q_BLOCK
    idx = jnp.maximum(jnp.arange(nd)[:, None] * Q_BLOCK + rel[None, :], 0)
    v = t[:, idx]
    flat = jnp.tile(v, (1, 1, Q_BLOCK))[:, :, :Q_BLOCK * (2 * Q_BLOCK - 1)]
    near = flat.reshape(heads, nd, Q_BLOCK, 2 * Q_BLOCK - 1)[:, :, :, :Q_BLOCK]
    far = jnp.broadcast_to(tab[RP_BUCKETS - 1].astype(F32)[:, None, None, None], (heads, 1, Q_BLOCK, Q_BLOCK))
    tiles = jnp.concatenate([near, far], axis=1)
    tiles = tiles.reshape(NSA_KV_HEADS, g, nd + 1, Q_BLOCK, Q_BLOCK).transpose(2, 0, 3, 1, 4)
    return tiles.reshape(nd + 1, NSA_KV_HEADS, Q_BLOCK, g * Q_BLOCK)


def _post_kernel(*refs, mode, n_attn, ff_chunk):
    it = iter(refs)
    x_ref = next(it)
    if mode == "even":
        oa_ref = next(it)
        stats = [(next(it), next(it), next(it)) for _ in range((n_attn - 1) // 3)]
    else:
        branch_refs = [next(it) for _ in range(n_attn)]
    wout_ref, g_ref, gt1_ref, sc2_ref, sh2_ref, gt2_ref, wup_ref, wdown_ref, o_ref = it

    if mode == "even":
        ms = [m_ref[...] for _, m_ref, _ in stats]
        big = functools.reduce(jnp.maximum, ms)
        l_tot = 0.0
        acc_tot = 0.0
        for (acc_ref, _, l_ref), m in zip(stats, ms):
            w = jnp.exp(m - big)
            l_tot = l_tot + l_ref[...] * w
            acc_tot = acc_tot + acc_ref[...] * w
        ob = acc_tot / l_tot
        o = jnp.concatenate([oa_ref[...], ob], axis=1)
    else:
        o = branch_refs[0][...]
        for br in branch_refs[1:]:
            o = o + br[...]

    y = _dot(o.astype(BF16), wout_ref[...])
    x1 = x_ref[...] + gt1_ref[...] * _rms(y, g_ref[1:2, :])
    h2 = (_rms(x1, g_ref[2:3, :]) * (1.0 + sc2_ref[...]) + sh2_ref[...]).astype(BF16)
    d_ff = wup_ref.shape[1]
    u = jnp.zeros(x1.shape, F32)
    for c0 in range(0, d_ff, ff_chunk):
        a = jnp.square(jnp.maximum(_dot(h2, wup_ref[:, c0:c0 + ff_chunk]), 0.0))
        u = u + _dot(a.astype(BF16), wdown_ref[c0:c0 + ff_chunk, :])
    o_ref[...] = x1 + gt2_ref[...] * _rms(u, g_ref[3:4, :])


def _post_block(x, attn, w_out, norm_g, mods, w_up, w_down, layer, *, mode, tm):
    b, l, d = x.shape
    r = mods[0].shape[1]
    rb = 1 if r == 1 else tm
    mod_map = (lambda bi, i: (bi, 0, 0)) if r == 1 else (lambda bi, i: (bi, i, 0))
    d_ff = w_up.shape[2]
    tok = lambda bi, i: (bi, i, 0)
    in_specs = [pl.BlockSpec((None, tm, d), tok)]
    in_specs += [pl.BlockSpec((None, tm, a.shape[2]), tok) for a in attn]
    in_specs += [pl.BlockSpec((None, d, d), lambda bi, i: (layer // 2, 0, 0)),
                 pl.BlockSpec((None, 4, d), lambda bi, i: (layer, 0, 0))]
    in_specs += [pl.BlockSpec((None, rb, d), mod_map)] * 4
    in_specs += [pl.BlockSpec((None, d, d_ff), lambda bi, i: (layer, 0, 0)),
                 pl.BlockSpec((None, d_ff, d), lambda bi, i: (layer, 0, 0))]
    return pl.pallas_call(
        functools.partial(_post_kernel, mode=mode, n_attn=len(attn), ff_chunk=1024),
        grid=(b, l // tm),
        in_specs=in_specs,
        out_specs=pl.BlockSpec((None, tm, d), tok),
        out_shape=jax.ShapeDtypeStruct((b, l, d), F32),
        compiler_params=_cparams("parallel", "parallel"),
        name="post_block",
    )(x, *attn, w_out, norm_g, *mods, w_up, w_down)


def _decode_cache_kernel(*refs, hq, hkv, mult_new, has_sink, has_gate, has_prev):
    it = iter(refs)
    q_ref, new_ref, c_ref, bias_ref, biasn_ref, mult_ref = [next(it) for _ in range(6)]
    sink_ref = next(it) if has_sink else None
    gate_ref = next(it) if has_gate else None
    if has_prev:
        next(it)
    o_ref, cout_ref = next(it), next(it)
    g = hq // hkv
    r = c_ref.shape[1]
    hd = HEAD_DIM
    s_rows, s_new = [], []
    for kh in range(hkv):
        kt = c_ref[kh * hd:(kh + 1) * hd, :]
        kn = new_ref[kh * hd:(kh + 1) * hd, :]
        for gi in range(g):
            h = kh * g + gi
            qc = q_ref[h * hd:(h + 1) * hd, :] * SCALE
            s_rows.append(jnp.sum(kt * qc, axis=0, keepdims=True))
            s_new.append(jnp.sum(kn * qc, axis=0, keepdims=True))
    mult = mult_ref[...]
    s = jnp.where(mult > 0.0, jnp.concatenate(s_rows, axis=0) + bias_ref[...], NEG)
    sn = jnp.concatenate(s_new, axis=0) + biasn_ref[...]
    m = jnp.maximum(jnp.max(s, axis=1, keepdims=True), sn)
    if has_sink:
        m = jnp.maximum(m, sink_ref[...])
    p = mult * jnp.exp(s - m)
    pn = mult_new * jnp.exp(sn - m)
    l = jnp.sum(p, axis=1, keepdims=True) + pn
    if has_sink:
        l = l + jnp.exp(sink_ref[...] - m)
    inv = 1.0 / l
    if has_gate:
        inv = inv * jax.nn.sigmoid(gate_ref[...])
    for kh in range(hkv):
        vt = c_ref[(hkv + kh) * hd:(hkv + kh + 1) * hd, :]
        vn = new_ref[(hkv + kh) * hd:(hkv + kh + 1) * hd, :]
        for gi in range(g):
            h = kh * g + gi
            o = jnp.sum(vt * p[h:h + 1, :], axis=1, keepdims=True) + vn * pn[h:h + 1, :]
            o_ref[h * hd:(h + 1) * hd, :] = o * inv[h:h + 1, :]
    lane = lax.broadcasted_iota(jnp.int32, (hd, r), 1)
    for c in range(2 * hkv):
        rows = slice(c * hd, (c + 1) * hd)
        shifted = pltpu.roll(c_ref[rows, :], r - 1, axis=1)
        cout_ref[rows, :] = jnp.where(lane == r - 1, new_ref[rows, :], shifted)


def _cache_view(cache):
    lyr, n, r = cache.shape[:3]
    return cache.transpose(0, 1, 3, 4, 5, 2).reshape(lyr, n, -1, r)


def _cache_unview(view, hkv):
    lyr, n, c, r = view.shape
    return view.reshape(lyr, n, 2, hkv, HEAD_DIM, r).transpose(0, 1, 5, 2, 3, 4)


def _decode_cache(q, new, view, layer, tab, patterns, *, hq, hkv, prev, sink=None, gate=None):
    lyr, n, c, wb = view.shape
    dist = wb - jnp.arange(wb)
    mult = sum(((dist <= w) & (dist % d == 0)).astype(F32) for w, d in patterns)[None, :]
    consts = [_bias_of(tab, dist), _bias_of(tab, jnp.zeros((1,), jnp.int32)), mult]
    if sink is not None:
        consts.append(sink.reshape(hq, 1).astype(F32))
    in_specs = [pl.BlockSpec((None, hq * HEAD_DIM, 1), lambda i: (i, 0, 0)),
                pl.BlockSpec((None, c, 1), lambda i: (i, 0, 0)),
                pl.BlockSpec((None, None, c, wb), lambda i: (layer, i, 0, 0))]
    in_specs += [pl.BlockSpec(cst.shape, lambda i: (0, 0)) for cst in consts]
    args = [q.reshape(n, hq * HEAD_DIM, 1), new.reshape(n, c, 1), view] + consts
    if gate is not None:
        in_specs.append(pl.BlockSpec((None, hq, 1), lambda i: (i, 0, 0)))
        args.append(gate.reshape(n, hq, 1))
    in_specs.append(pl.BlockSpec(memory_space=pl.ANY))
    aliases = {len(args): 1}
    args.append(prev)
    o, buf = pl.pallas_call(
        functools.partial(_decode_cache_kernel, hq=hq, hkv=hkv, mult_new=float(len(patterns)),
                          has_sink=sink is not None, has_gate=gate is not None, has_prev=True),
        grid=(n,),
        in_specs=in_specs,
        out_specs=[pl.BlockSpec((None, hq * HEAD_DIM, 1), lambda i: (i, 0, 0)),
                   pl.BlockSpec((None, None, c, wb), lambda i: (layer, i, 0, 0))],
        out_shape=[jax.ShapeDtypeStruct((n, hq * HEAD_DIM, 1), F32),
                   jax.ShapeDtypeStruct((lyr, n, c, wb), F32)],
        input_output_aliases=aliases,
        compiler_params=_cparams("arbitrary"),
        name="decode_cache",
    )(*args)
    return o.reshape(n, hq * HEAD_DIM), buf


def _paged_hab_kernel(pt_ref, *refs, n_pages):
    page_refs = refs[:n_pages]
    perm_ref, w_ref, o_ref, z_ref = refs[n_pages:]
    page = page_refs[0].shape[1]
    cpp = page // CMP_STRIDE
    hw = w_ref.shape[3] // 2
    perm = perm_ref[...]
    for pg in range(n_pages):
        xp = _dot_nt(perm, page_refs[pg][...].astype(BF16))
        for s in range(CMP_STRIDE):
            z_ref[s, pg * cpp:(pg + 1) * cpp, :] = xp[s * cpp:(s + 1) * cpp, :]
    for p in range(2):
        acc = None
        for s in range(CMP_STRIDE):
            part = _dot(z_ref[s, :, p * LANES:(p + 1) * LANES].astype(BF16), w_ref[s, p])
            acc = part if acc is None else acc + part
        o_ref[:, p * hw:(p + 1) * hw] = acc[:, :hw]
        o_ref[:, (2 + p) * hw:(3 + p) * hw] = acc[:, hw:]


def _paged_hab(pool_view, page_table, w1p, layer):
    lyr, n_pool, c, page = pool_view.shape
    n, n_pages = page_table.shape
    cpp = page // CMP_STRIDE
    nch = n_pages * cpp
    nout = 2 * w1p.shape[4]
    rows = jnp.arange(page)
    perm = (rows[None, :] == (rows[:, None] % cpp) * CMP_STRIDE + rows[:, None] // cpp).astype(BF16)

    def page_map(k):
        return lambda i, pt: (layer, pt[i, k], 0, 0)

    grid_spec = pltpu.PrefetchScalarGridSpec(
        num_scalar_prefetch=1,
        grid=(n,),
        in_specs=[pl.BlockSpec((None, None, c, page), page_map(k)) for k in range(n_pages)]
        + [pl.BlockSpec(perm.shape, lambda i, pt: (0, 0)),
           pl.BlockSpec((None,) + w1p.shape[1:], lambda i, pt: (layer, 0, 0, 0, 0))],
        out_specs=pl.BlockSpec((None, nch, nout), lambda i, pt: (i, 0, 0)),
        scratch_shapes=[pltpu.VMEM((CMP_STRIDE, nch, c), F32)],
    )
    return pl.pallas_call(
        functools.partial(_paged_hab_kernel, n_pages=n_pages),
        grid_spec=grid_spec,
        out_shape=jax.ShapeDtypeStruct((n, nch, nout), F32),
        compiler_params=_cparams("arbitrary"),
        name="paged_hab",
    )(page_table, *([pool_view] * n_pages), perm, w1p)


def _cmp_sample_kernel(q_ref, ckv_ref, bias_ref, ov_ref, gate_ref, o_ref, idx_ref, *, nch, n_slc, qpos):
    g = NSA_HEADS // NSA_KV_HEADS
    wk = NSA_KV_HEADS * HEAD_DIM
    ckv = ckv_ref[...]
    c = lax.broadcasted_iota(jnp.int32, (g, nch), 1)
    mask = (c * CMP_STRIDE + CMP_LEN - 1 <= qpos) & (c < nch - 1)
    q = q_ref[...] * SCALE
    gates = jax.nn.sigmoid(gate_ref[...])
    n_slc_pad = ov_ref.shape[1]
    jb = lax.broadcasted_iota(jnp.int32, (1, n_slc_pad), 1)
    cur = qpos // SLC_BLOCK
    valid = (jb * SLC_BLOCK <= qpos) & (jb < n_slc)
    forced = (jb == 0) | (jb == cur) | (jb == cur - 1)
    k_sel = min(N_SELECT, n_slc)
    lane = lax.broadcasted_iota(jnp.int32, (1, LANES), 1)
    for kh in range(NSA_KV_HEADS):
        rs = slice(kh * g, (kh + 1) * g)
        k_h = ckv[:, kh * HEAD_DIM:(kh + 1) * HEAD_DIM]
        v_h = ckv[:, wk + kh * HEAD_DIM: wk + (kh + 1) * HEAD_DIM].astype(BF16)
        s = _dot_nt_3pass(q[rs, :], k_h) + bias_ref[rs, :]
        s = jnp.where(mask, s, NEG)
        m = jnp.max(s, axis=-1, keepdims=True)
        p = jnp.where(mask, jnp.exp(s - m), 0.0)
        p = p / jnp.maximum(jnp.sum(p, axis=-1, keepdims=True), 1e-30)
        o_ref[rs, :] = _dot(p.astype(BF16), v_h) * gates[rs, 0:1]
        imp = jnp.sum(_dot_2pass_lhs(p, ov_ref[...]), axis=0, keepdims=True)
        imp = jnp.where(valid, imp + jnp.where(forced, FORCE, 0.0), NEG)
        _, picks = _topk_select(imp, k_sel)
        row = jnp.zeros((1, LANES), jnp.int32)
        for t, idx in enumerate(picks):
            row = jnp.where(lane == t, idx.astype(jnp.int32), row)
        idx_ref[kh:kh + 1, :] = row


def _cmp_sample(q, ckv, bias, ov, gate, *, n_slc, qpos):
    n = q.shape[0]
    nch = ckv.shape[1]
    return pl.pallas_call(
        functools.partial(_cmp_sample_kernel, nch=nch, n_slc=n_slc, qpos=qpos),
        grid=(n,),
        in_specs=[pl.BlockSpec((None, NSA_HEADS, HEAD_DIM), lambda i: (i, 0, 0)),
                  pl.BlockSpec((None, nch, ckv.shape[2]), lambda i: (i, 0, 0)),
                  pl.BlockSpec(bias.shape, lambda i: (0, 0)),
                  pl.BlockSpec(ov.shape, lambda i: (0, 0)),
                  pl.BlockSpec((None, NSA_HEADS, 3), lambda i: (i, 0, 0))],
        out_specs=[pl.BlockSpec((None, NSA_HEADS, HEAD_DIM), lambda i: (i, 0, 0)),
                   pl.BlockSpec((None, NSA_KV_HEADS, LANES), lambda i: (i, 0, 0))],
        out_shape=[jax.ShapeDtypeStruct((n, NSA_HEADS, HEAD_DIM), F32),
                   jax.ShapeDtypeStruct((n, NSA_KV_HEADS, LANES), jnp.int32)],
        compiler_params=_cparams("parallel"),
        name="cmp_sample",
    )(q, ckv, bias, ov, gate)


def _slc_sample_kernel(pg_ref, idx_ref, *refs, k_sel, nb_past):
    n_sel = NSA_KV_HEADS * k_sel
    page_refs = refs[:n_sel]
    q_ref, new_ref, bias_ref, biasn_ref, gate_ref, o_ref = refs[n_sel:]
    i = pl.program_id(0)
    g = NSA_HEADS // NSA_KV_HEADS
    wk = NSA_KV_HEADS * HEAD_DIM
    page = page_refs[0].shape[1]
    bpp = page // SLC_BLOCK
    lane_blk = lax.broadcasted_iota(jnp.int32, (g, page), 1) // SLC_BLOCK
    for kh in range(NSA_KV_HEADS):
        rs = slice(kh * g, (kh + 1) * g)
        ks = slice(kh * HEAD_DIM, (kh + 1) * HEAD_DIM)
        vs = slice(wk + kh * HEAD_DIM, wk + (kh + 1) * HEAD_DIM)
        qg = q_ref[rs, :] * SCALE
        sn = jnp.sum(qg * new_ref[:, ks], axis=-1, keepdims=True) + biasn_ref[rs, :]
        m = sn
        scores = []
        for t in range(k_sel):
            idx = idx_ref[(i * NSA_KV_HEADS + kh) * k_sel + t]
            lp = jnp.minimum(idx, nb_past - 1) // bpp
            kt = page_refs[kh * k_sel + t][ks, :].astype(BF16)
            s = _dot(qg.astype(BF16), kt) + bias_ref[lp, rs, :]
            ok = (lane_blk == idx % bpp) & (idx < nb_past)
            s = jnp.where(ok, s, NEG)
            m = jnp.maximum(m, jnp.max(s, axis=-1, keepdims=True))
            scores.append((s, ok))
        l = jnp.exp(sn - m)
        acc = l * new_ref[:, vs]
        for t in range(k_sel):
            s, ok = scores[t]
            p = jnp.where(ok, jnp.exp(s - m), 0.0)
            l = l + jnp.sum(p, axis=-1, keepdims=True)
            vt = page_refs[kh * k_sel + t][vs, :].astype(BF16)
            acc = acc + _dot_nt(p.astype(BF16), vt)
        o_ref[rs, :] = acc / l * jax.nn.sigmoid(gate_ref[rs, :])


def _slc_sample(q, new, pool_view, layer, pages, idx, bias, bias_new, gate, *, nb_past):
    n = q.shape[0]
    lyr, n_pool, c, page = pool_view.shape
    k_sel = pages.shape[0] // (n * NSA_KV_HEADS)
    n_sel = NSA_KV_HEADS * k_sel

    def page_map(k):
        return lambda i, pg, ids: (layer, pg[i * n_sel + k], 0, 0)

    grid_spec = pltpu.PrefetchScalarGridSpec(
        num_scalar_prefetch=2,
        grid=(n,),
        in_specs=[pl.BlockSpec((None, None, c, page), page_map(k)) for k in range(n_sel)]
        + [pl.BlockSpec((None, NSA_HEADS, HEAD_DIM), lambda i, pg, ids: (i, 0, 0)),
           pl.BlockSpec((None, 1, c), lambda i, pg, ids: (i, 0, 0)),
           pl.BlockSpec(bias.shape, lambda i, pg, ids: (0, 0, 0)),
           pl.BlockSpec(bias_new.shape, lambda i, pg, ids: (0, 0)),
           pl.BlockSpec((None, NSA_HEADS, 1), lambda i, pg, ids: (i, 0, 0))],
        out_specs=pl.BlockSpec((None, NSA_HEADS, HEAD_DIM), lambda i, pg, ids: (i, 0, 0)),
    )
    return pl.pallas_call(
        functools.partial(_slc_sample_kernel, k_sel=k_sel, nb_past=nb_past),
        grid_spec=grid_spec,
        out_shape=jax.ShapeDtypeStruct((n, NSA_HEADS, HEAD_DIM), F32),
        compiler_params=_cparams("arbitrary"),
        name="slc_sample",
    )(pages, idx, *([pool_view] * n_sel), q, new, bias, bias_new, gate)


EVEN_SPLITS = (A_HEADS * HEAD_DIM, 2 * A_KV_HEADS * HEAD_DIM, B_HEADS * HEAD_DIM, 2 * B_HEADS * HEAD_DIM)
NSA_KV_W = 2 * NSA_KV_HEADS * HEAD_DIM
NSA_SPLITS = (NSA_HEADS * HEAD_DIM, NSA_KV_W, NSA_KV_W, NSA_KV_W, LANES)
NSA_SPLITS_T = (NSA_KV_HEADS * HEAD_DIM, 3 * NSA_HEADS)


def _mod_parts(mod):
    if mod.ndim == 3:
        return [mod[:, i:i + 1] for i in range(6)]
    return [mod[:, :, i] for i in range(6)]


def _even_layer(xp, xs, mod_p, mod_s, view_swa, view_dil, bufs, li, layer, wts, tabs):
    w_in, w_out, sinks, norm_g, w_up, w_down = wts
    g0 = norm_g[layer, 0:1]
    sh1, sc1, gt1, sh2, sc2, gt2 = _mod_parts(mod_p)
    qa, kva, qb, kvb = _norm_linear(xp, g0, sc1, sh1, w_in, li, EVEN_SPLITS, tm=512)
    (oa,) = _banded_attn(qa, kva, tabs["swa"], hq=A_HEADS, hkv=A_KV_HEADS, window=WIN_A, sink=sinks[li])
    stats = []
    for (w, d), bias in zip(DIL_PAIRS, tabs["dil"]):
        stats += _banded_attn(qb, kvb, bias, hq=B_HEADS, hkv=B_HEADS, window=w // d, dil=d, mode="stats")
    xp_new = _post_block(xp, [oa] + stats, w_out, norm_g, (gt1, sc2, sh2, gt2), w_up, w_down, layer,
                         mode="even", tm=256)
    l = xp.shape[1]
    state_p = (kva[:, l - min(WIN_A, l):], kvb[:, l - min(DIL_MAX_WIN, l):])
    sh1, sc1, gt1, sh2, sc2, gt2 = _mod_parts(mod_s)
    n = xs.shape[1]
    qa, kva, qb, kvb = [a[0] for a in _norm_linear(xs, g0, sc1, sh1, w_in, li, EVEN_SPLITS, tm=n)]
    oa, bufs["swa"] = _decode_cache(qa, kva, view_swa, li, tabs["rel"][:, :A_HEADS], ((WIN_A, 1),),
                                    hq=A_HEADS, hkv=A_KV_HEADS, prev=bufs["swa"], sink=sinks[li])
    ob, bufs["dil"] = _decode_cache(qb, kvb, view_dil, li, tabs["rel"][:, A_HEADS:], DIL_PAIRS,
                                    hq=B_HEADS, hkv=B_HEADS, prev=bufs["dil"])
    o = jnp.concatenate([oa, ob], axis=1)[None]
    xs_new = _post_block(xs, [o], w_out, norm_g, (gt1, sc2, sh2, gt2), w_up, w_down, layer,
                         mode="sum", tm=n)
    return xp_new, xs_new, state_p


def _nsa_layer(xp, xs, mod_p, mod_s, view_cmp, view_slc, view_win, bufs, page_table, li, layer, wts, tabs):
    w_in, w_in_t, w_out, w1big, w2big, w1p, pe_rows, norm_g, w_up, w_down = wts
    g0 = norm_g[layer, 0:1]
    b, l, _ = xp.shape
    pe_h = _linear(pe_rows[li], w1big, li, tm=8, tn=512)
    sh1, sc1, gt1, sh2, sc2, gt2 = _mod_parts(mod_p)
    q, kv_cmp, kv_slc, kv_win, gate, vt_slc, gate_t = _norm_linear(
        xp, g0, sc1, sh1, w_in, li, NSA_SPLITS, tm=512, w_t=w_in_t, splits_t=NSA_SPLITS_T)
    nch = l // CMP_STRIDE
    n_slc = -(-l // SLC_BLOCK)
    hab = _linear(kv_cmp.reshape(b * nch, CMP_STRIDE * NSA_KV_W), w1big, li, tm=min(512, b * nch), tn=512)
    pad = nch - Q_BLOCK // CMP_STRIDE
    ckv_pad = _compress_finish(hab.reshape(b, nch, -1), pe_h, w2big, li, pad=pad, rows_out=2 * nch)
    o_cmp, sel = _cmp_prompt(q, ckv_pad, tabs["cmp_rel"], tabs["ov_pad"], gate, nch=nch, n_slc=n_slc)
    o_slc = _slc_prompt(q, kv_slc, vt_slc, sel, tabs["slc_tiles"], gate_t)
    (o_win,) = _banded_attn(q, kv_win, tabs["win"], hq=NSA_HEADS, hkv=NSA_KV_HEADS, window=WIN_C,
                            gate=gate, gate_col=2)
    xp_new = _post_block(xp, [o_cmp, o_slc, o_win], w_out, norm_g, (gt1, sc2, sh2, gt2), w_up, w_down,
                         layer, mode="sum", tm=256)
    state_p = (kv_cmp, kv_slc, kv_win[:, l - min(WIN_C, l):])
    sh1, sc1, gt1, sh2, sc2, gt2 = _mod_parts(mod_s)
    n = xs.shape[1]
    q, kv_cmp_s, kv_slc_s, kv_win_s, gate = [a[0] for a in
                                             _norm_linear(xs, g0, sc1, sh1, w_in, li, NSA_SPLITS, tm=n)]
    n_pages = page_table.shape[1]
    page = view_cmp.shape[3]
    past = n_pages * page
    assert past % SLC_BLOCK == 0 and past % CMP_STRIDE == 0
    nch_s = past // CMP_STRIDE
    nb_past = past // SLC_BLOCK
    n_slc_s = nb_past + 1
    hab_s = _paged_hab(view_cmp, page_table, w1p, li)
    ckv_s = _compress_finish(hab_s, pe_h, w2big, li, pad=0, rows_out=nch_s)
    gate3 = gate[:, :NSA_HEADS * 3].reshape(n, NSA_HEADS, 3)
    o_cmp, idx = _cmp_sample(q.reshape(n, NSA_HEADS, HEAD_DIM), ckv_s, tabs["cmp_s"], tabs["ov_s"], gate3,
                             n_slc=n_slc_s, qpos=past)
    k_sel = min(N_SELECT, n_slc_s)
    idx = idx[:, :, :k_sel]
    bpp = page // SLC_BLOCK
    pidx = jnp.clip(idx, 0, nb_past - 1)
    phys = jnp.take_along_axis(page_table, (pidx // bpp).reshape(n, -1), axis=1)
    o_slc = _slc_sample(q.reshape(n, NSA_HEADS, HEAD_DIM), kv_slc_s[:, None], view_slc, li, phys.reshape(-1),
                        idx.reshape(-1), tabs["slc_s"], tabs["slc_s_new"], gate3[:, :, 1:2], nb_past=nb_past)
    o_win, bufs["win"] = _decode_cache(q, kv_win_s, view_win, li, tabs["rel"], ((WIN_C, 1),), hq=NSA_HEADS,
                                       hkv=NSA_KV_HEADS, prev=bufs["win"], gate=gate3[:, :, 2])
    attn = [o_cmp.reshape(1, n, -1), o_slc.reshape(1, n, -1), o_win[None]]
    xs_new = _post_block(xs, attn, w_out, norm_g, (gt1, sc2, sh2, gt2), w_up, w_down, layer, mode="sum", tm=n)
    return xp_new, xs_new, state_p, (kv_cmp_s, kv_slc_s)


def kernel(x_prompt, x_sample, cache_swa_kv, cache_dil_kv, cache_nsa_cmp, cache_nsa_slc, cache_nsa_win,
           page_table, c_prompt, c_sample, rel_bias, ada_w, ada_b, norm_g, mlp_up, mlp_down,
           even_w_in, even_w_out, attn_sinks, nsa_w_in, nsa_w_out, cmp_w1, cmp_w2, cmp_pe):
    depth, d, _ = ada_w.shape
    b, l, _ = x_prompt.shape
    n = x_sample.shape[0]
    assert x_sample.shape[1] == 1
    assert cache_swa_kv.shape[2] == WIN_A and cache_dil_kv.shape[2] == DIL_MAX_WIN and cache_nsa_win.shape[2] == WIN_C

    ada_wb = ada_w.astype(BF16)
    up_b = mlp_up.astype(BF16)
    down_b = mlp_down.astype(BF16)
    even_in_b = even_w_in.astype(BF16)
    even_out_b = even_w_out.astype(BF16)
    nsa_in_b = jnp.pad(nsa_w_in, ((0, 0), (0, 0), (0, sum(NSA_SPLITS) - nsa_w_in.shape[2]))).astype(BF16)
    nsa_out_b = nsa_w_out.astype(BF16)
    q_w = NSA_HEADS * HEAD_DIM
    v_slc_cols = nsa_w_in[:, :, q_w + NSA_KV_W + NSA_KV_W // 2: q_w + 2 * NSA_KV_W]
    gate_cols = nsa_w_in[:, :, q_w + 3 * NSA_KV_W:].reshape(-1, d, NSA_HEADS, 3).transpose(0, 1, 3, 2)
    nsa_in_t = jnp.concatenate([v_slc_cols, gate_cols.reshape(-1, d, 3 * NSA_HEADS)], axis=2)
    nsa_in_t = nsa_in_t.transpose(0, 2, 1).astype(BF16)
    w1big, w2big, w1p = _compress_weights(cmp_w1, cmp_w2)
    pe_rows = _pe_rows(cmp_pe)

    c_all = jnp.concatenate([c_prompt, c_sample], axis=0)
    mods = [_linear(c_all, ada_wb, layer, ada_b[:, None, :], pre="silu", tm=b + n, tn=1536)
            for layer in range(depth)]

    nq = l // Q_BLOCK
    nch = l // CMP_STRIDE
    n_slc_pad = _round_up(-(-l // SLC_BLOCK), LANES)
    past = page_table.shape[1] * cache_nsa_cmp.shape[2]
    nch_s = past // CMP_STRIDE
    nb_past = past // SLC_BLOCK
    w_rel = jnp.arange(nch) - (nch - Q_BLOCK // CMP_STRIDE)
    cmp_rel_dist = jnp.arange(Q_BLOCK)[:, None] - CMP_STRIDE * w_rel[None, :] - (CMP_LEN - 1)
    ov = _overlap_table(nch, n_slc_pad)
    pad = nch - Q_BLOCK // CMP_STRIDE
    page = cache_nsa_slc.shape[2]
    slc_s_dist = past - (jnp.arange(past // page)[:, None] * page + jnp.arange(page)[None, :])
    tabs = {
        "rel": rel_bias,
        "swa": _banded_bias(rel_bias[:, :A_HEADS], WIN_A, 1),
        "dil": [_banded_bias(rel_bias[:, A_HEADS:], w // dd, dd) for w, dd in DIL_PAIRS],
        "win": _banded_bias(rel_bias, WIN_C, 1),
        "cmp_rel": _bias_of(rel_bias, cmp_rel_dist),
        "ov_pad": jnp.pad(ov, ((pad, 2 * nch - pad - nch), (0, 0))),
        "slc_tiles": _slc_bias_tiles(rel_bias, nq),
        "cmp_s": _bias_of(rel_bias, past - (jnp.arange(nch_s) * CMP_STRIDE + CMP_LEN - 1)),
        "ov_s": _overlap_table(nch_s, _round_up(nb_past + 1, LANES)),
        "slc_s": _bias_of(rel_bias, slc_s_dist).transpose(1, 0, 2),
        "slc_s_new": _bias_of(rel_bias, jnp.zeros((1,), jnp.int32)),
    }
    views = {"swa": _cache_view(cache_swa_kv), "dil": _cache_view(cache_dil_kv), "win": _cache_view(cache_nsa_win),
             "cmp": _cache_view(cache_nsa_cmp), "slc": _cache_view(cache_nsa_slc)}
    bufs = {k: jnp.zeros(views[k].shape, F32) for k in ("swa", "dil", "win")}

    xp = x_prompt
    xs = x_sample.reshape(1, n, d)
    outs = {k: [] for k in ("swa_p", "dil_p", "cmp_p", "cmp_s", "slc_p", "slc_s", "win_p")}
    for layer in range(depth):
        li = layer // 2
        mod = mods[layer].reshape(b + n, 6, d)
        mod_p, mod_s = mod[:b], mod[b:][None]
        if layer % 2 == 0:
            wts = (even_in_b, even_out_b, attn_sinks, norm_g, up_b, down_b)
            xp, xs, (sa, sb) = _even_layer(xp, xs, mod_p, mod_s, views["swa"], views["dil"], bufs, li,
                                           layer, wts, tabs)
            outs["swa_p"].append(sa.reshape(b, -1, 2, A_KV_HEADS, HEAD_DIM))
            outs["dil_p"].append(sb.reshape(b, -1, 2, B_HEADS, HEAD_DIM))
        else:
            wts = (nsa_in_b, nsa_in_t, nsa_out_b, w1big, w2big, w1p, pe_rows, norm_g, up_b, down_b)
            xp, xs, (pc, ps, pw), (qc, qs_) = _nsa_layer(xp, xs, mod_p, mod_s, views["cmp"], views["slc"],
                                                         views["win"], bufs, page_table, li, layer, wts, tabs)
            kvshape = (2, NSA_KV_HEADS, HEAD_DIM)
            outs["cmp_p"].append(pc.reshape(b, -1, *kvshape))
            outs["slc_p"].append(ps.reshape(b, -1, *kvshape))
            outs["win_p"].append(pw.reshape(b, -1, *kvshape))
            outs["cmp_s"].append(qc.reshape(n, 1, *kvshape))
            outs["slc_s"].append(qs_.reshape(n, 1, *kvshape))
    st = {k: jnp.stack(v) for k, v in outs.items()}
    swa_s = _cache_unview(bufs["swa"], A_KV_HEADS)
    dil_s = _cache_unview(bufs["dil"], B_HEADS)
    win_s = _cache_unview(bufs["win"], NSA_KV_HEADS)
    return (xp, xs.reshape(n, 1, d), st["swa_p"], swa_s, st["dil_p"], dil_s,
            st["cmp_p"], st["cmp_s"], st["slc_p"], st["slc_s"], st["win_p"], win_s)
```

```python
import functools
import math

import jax
import jax.numpy as jnp
from jax import lax
from jax.experimental import pallas as pl
from jax.experimental.pallas import tpu as pltpu

F32 = jnp.float32
BF16 = jnp.bfloat16

HEAD_DIM = 64
A_HEADS = 8
A_KV_HEADS = 2
B_HEADS = 8
WIN_A = 128
DIL_PAIRS = ((128, 1), (512, 4), (2048, 16))
DIL_MAX_WIN = 2048
NSA_HEADS = 16
NSA_KV_HEADS = 2
CMP_STRIDE = 16
CMP_LEN = 32
CMP_HID = 128
SLC_BLOCK = 64
N_SELECT = 16
WIN_C = 512
RP_BUCKETS = 32
RP_MAX_DIST = 2048
Q_BLOCK = 128
EPS = 1e-6
NEG = -1e30
REMOVED = -3e38
FORCE = 1e6
SCALE = HEAD_DIM ** -0.5
CMP_SAMPLE_GROUP = 8
SLC_TILES_PER_STEP = 4

VMEM_LIMIT_BYTES = 56 * 1024 * 1024
LANES = 128


def _cparams(*sem):
    return pltpu.CompilerParams(dimension_semantics=sem, vmem_limit_bytes=VMEM_LIMIT_BYTES)


def _round_up(x, m):
    return (x + m - 1) // m * m


def _t5_bucket(dist):
    n = dist.astype(jnp.int32)
    exact = RP_BUCKETS // 2
    val = jnp.log(jnp.maximum(n, 1).astype(F32) / exact) / math.log(RP_MAX_DIST / exact)
    large = jnp.minimum(exact + (val * (RP_BUCKETS - exact)).astype(jnp.int32), RP_BUCKETS - 1)
    return jnp.where(n < exact, n, large)


def _bias_of(tab, dist):
    b = tab[_t5_bucket(jnp.maximum(dist, 0))].astype(F32)
    return jnp.moveaxis(b, -1, 0)


def _toeplitz(v, rows, cols):
    lv = v.shape[-1]
    assert lv > cols and lv >= rows + cols - 1
    flat = jnp.tile(v, (1,) * (v.ndim - 1) + (rows,))[..., :rows * (lv - 1)]
    return flat.reshape(v.shape[:-1] + (rows, lv - 1))[..., :cols]


def _signed_index(lv, n_pos):
    u = jnp.arange(lv)
    return jnp.where(u < n_pos, u, u - lv)


def _rms(x, g):
    return x * lax.rsqrt(jnp.mean(x * x, axis=-1, keepdims=True) + EPS) * g


def _dot(a, b):
    return jnp.dot(a, b, preferred_element_type=F32)


def _dot_nt(a, b):
    return lax.dot_general(a, b, (((1,), (1,)), ((), ())), preferred_element_type=F32)


def _split_bf16(x):
    hi = x.astype(BF16)
    lo = (x - hi.astype(F32)).astype(BF16)
    return hi, lo


def _dot_nt_3pass(a, b):
    ah, al = _split_bf16(a)
    bh, bl = _split_bf16(b)
    return _dot_nt(ah, bh) + (_dot_nt(ah, bl) + _dot_nt(al, bh))


def _dot_2pass_lhs(a, b_bf16):
    ah, al = _split_bf16(a)
    return _dot(ah, b_bf16) + _dot(al, b_bf16)


def _gelu_tanh(x):
    return 0.5 * x * (1.0 + jnp.tanh(math.sqrt(2.0 / math.pi) * (x + 0.044715 * (x * x * x))))


def _linear_kernel(*refs, pre, has_bias):
    if has_bias:
        a_ref, b_ref, bias_ref, o_ref = refs
    else:
        a_ref, b_ref, o_ref = refs
    a = a_ref[...]
    if pre == "silu":
        a = a * jax.nn.sigmoid(a)
    acc = _dot(a.astype(BF16), b_ref[...])
    if has_bias:
        acc = acc + bias_ref[...]
    o_ref[...] = acc


def _linear(a, b, layer, bias=None, *, pre=None, tm, tn):
    m, k = a.shape
    n = b.shape[2]
    assert m % tm == 0 and n % tn == 0
    in_specs = [pl.BlockSpec((tm, k), lambda i, j: (i, 0)),
                pl.BlockSpec((None, k, tn), lambda i, j: (layer, 0, j))]
    args = [a, b]
    if bias is not None:
        in_specs.append(pl.BlockSpec((None, 1, tn), lambda i, j: (layer, 0, j)))
        args.append(bias)
    return pl.pallas_call(
        functools.partial(_linear_kernel, pre=pre, has_bias=bias is not None),
        grid=(m // tm, n // tn),
        in_specs=in_specs,
        out_specs=pl.BlockSpec((tm, tn), lambda i, j: (i, j)),
        out_shape=jax.ShapeDtypeStruct((m, n), F32),
        compiler_params=_cparams("parallel", "parallel"),
        name="linear",
    )(*args)


def _norm_linear_kernel(*refs, splits, splits_t):
    x_ref, g_ref, sc_ref, sh_ref, w_ref = refs[:5]
    rest = refs[5:]
    if splits_t:
        wt_ref, rest = rest[0], rest[1:]
    o_refs = rest[:len(splits)]
    ot_refs = rest[len(splits):]
    h = _rms(x_ref[...], g_ref[...]) * (1.0 + sc_ref[...]) + sh_ref[...]
    hb = h.astype(BF16)
    off = 0
    for o_ref, sz in zip(o_refs, splits):
        o_ref[...] = _dot(hb, w_ref[:, off:off + sz])
        off += sz
    off = 0
    for o_ref, sz in zip(ot_refs, splits_t):
        o_ref[...] = _dot_nt(wt_ref[off:off + sz, :], hb)
        off += sz


def _norm_linear(x, g, sc, sh, w, layer, splits, *, tm, w_t=None, splits_t=()):
    b, l, d = x.shape
    n = w.shape[2]
    assert sum(splits) == n and l % tm == 0
    r = sc.shape[1]
    rb = 1 if r == 1 else tm
    mod_map = (lambda bi, i: (bi, 0, 0)) if r == 1 else (lambda bi, i: (bi, i, 0))
    in_specs = [pl.BlockSpec((None, tm, d), lambda bi, i: (bi, i, 0)),
                pl.BlockSpec((1, d), lambda bi, i: (0, 0)),
                pl.BlockSpec((None, rb, d), mod_map),
                pl.BlockSpec((None, rb, d), mod_map),
                pl.BlockSpec((None, d, n), lambda bi, i: (layer, 0, 0))]
    args = [x, g, sc, sh, w]
    if splits_t:
        assert sum(splits_t) == w_t.shape[1]
        in_specs.append(pl.BlockSpec((None, w_t.shape[1], d), lambda bi, i: (layer, 0, 0)))
        args.append(w_t)
    return pl.pallas_call(
        functools.partial(_norm_linear_kernel, splits=splits, splits_t=splits_t),
        grid=(b, l // tm),
        in_specs=in_specs,
        out_specs=[pl.BlockSpec((None, tm, sz), lambda bi, i: (bi, i, 0)) for sz in splits]
        + [pl.BlockSpec((None, sz, tm), lambda bi, i: (bi, 0, i)) for sz in splits_t],
        out_shape=[jax.ShapeDtypeStruct((b, l, sz), F32) for sz in splits]
        + [jax.ShapeDtypeStruct((b, sz, l), F32) for sz in splits_t],
        compiler_params=_cparams("parallel", "parallel"),
        name="norm_linear",
    )(*args)


def _banded_kernel(*refs, hq, hkv, nb, window, mode, has_sink, gate_col):
    it = iter(refs)
    q_ref = next(it)
    k_refs = [next(it) for _ in range(nb + 1)]
    v_refs = [next(it) for _ in range(nb + 1)]
    bias_ref = next(it)
    sink_ref = next(it) if has_sink else None
    gate_ref = next(it) if gate_col is not None else None
    outs = list(it)
    g = hq // hkv
    kw = (nb + 1) * Q_BLOCK
    i = pl.program_id(2)

    r = lax.broadcasted_iota(jnp.int32, (Q_BLOCK, kw), 0)
    c = lax.broadcasted_iota(jnp.int32, (Q_BLOCK, kw), 1)
    dist = r + nb * Q_BLOCK - c
    kpos = (i - nb) * Q_BLOCK + c
    mask = (dist >= 0) & (dist <= window) & (kpos >= 0)

    q = q_ref[...] * SCALE
    kcat = jnp.concatenate([kr[...] for kr in k_refs], axis=0).astype(BF16)
    vcat = jnp.concatenate([vr[...] for vr in v_refs], axis=0).astype(BF16)
    if gate_ref is not None:
        gates = jax.nn.sigmoid(gate_ref[...])
    for h in range(hq):
        kh = h // g
        hs = slice(h * HEAD_DIM, (h + 1) * HEAD_DIM)
        ks = slice(kh * HEAD_DIM, (kh + 1) * HEAD_DIM)
        s = _dot_nt(q[:, hs].astype(BF16), kcat[:, ks]) + bias_ref[h]
        s = jnp.where(mask, s, NEG)
        m = jnp.max(s, axis=-1, keepdims=True)
        p = jnp.exp(s - m)
        l = jnp.sum(p, axis=-1, keepdims=True)
        acc = _dot(p.astype(BF16), vcat[:, ks])
        if mode == "stats":
            acc_ref, m_ref, l_ref = outs
            acc_ref[:, hs] = acc
            m_ref[:, hs] = jnp.broadcast_to(m, (Q_BLOCK, HEAD_DIM))
            l_ref[:, hs] = jnp.broadcast_to(l, (Q_BLOCK, HEAD_DIM))
        else:
            (o_ref,) = outs
            if has_sink:
                sk = sink_ref[h]
                big = jnp.maximum(m, sk)
                w = jnp.exp(m - big)
                o = (acc * w) / (l * w + jnp.exp(sk - big))
            else:
                o = acc / l
            if gate_ref is not None:
                col = h * 3 + gate_col
                o = o * gates[:, col:col + 1]
            o_ref[:, hs] = o


def _banded_attn(q, kv, bias, *, hq, hkv, window, dil=1, mode="norm", sink=None,
                 gate=None, gate_col=None):
    b, l, wq = q.shape
    wk = hkv * HEAD_DIM
    assert wq == hq * HEAD_DIM and kv.shape == (b, l, 2 * wk)
    ls = l // dil
    assert l % dil == 0 and ls % Q_BLOCK == 0
    nq = ls // Q_BLOCK
    nb = -(-window // Q_BLOCK)
    qv = q.reshape(b, ls, dil * wq)
    kvv = kv.reshape(b, ls, dil * 2 * wk)

    def kmap(j, half):
        return lambda bi, r, i: (bi, jnp.maximum(i - nb + j, 0), 2 * r + half)

    in_specs = [pl.BlockSpec((None, Q_BLOCK, wq), lambda bi, r, i: (bi, i, r))]
    args = [qv]
    for half in (0, 1):
        for j in range(nb + 1):
            in_specs.append(pl.BlockSpec((None, Q_BLOCK, wk), kmap(j, half)))
            args.append(kvv)
    in_specs.append(pl.BlockSpec(bias.shape, lambda bi, r, i: (0, 0, 0)))
    args.append(bias)
    if sink is not None:
        in_specs.append(pl.BlockSpec(memory_space=pltpu.SMEM))
        args.append(sink)
    if gate is not None:
        assert dil == 1
        in_specs.append(pl.BlockSpec((None, Q_BLOCK, gate.shape[2]), lambda bi, r, i: (bi, i, 0)))
        args.append(gate)
    n_out = 3 if mode == "stats" else 1
    out = pl.pallas_call(
        functools.partial(_banded_kernel, hq=hq, hkv=hkv, nb=nb, window=window, mode=mode,
                          has_sink=sink is not None, gate_col=gate_col if gate is not None else None),
        grid=(b, dil, nq),
        in_specs=in_specs,
        out_specs=[pl.BlockSpec((None, Q_BLOCK, wq), lambda bi, r, i: (bi, i, r))] * n_out,
        out_shape=[jax.ShapeDtypeStruct((b, ls, dil * wq), F32)] * n_out,
        compiler_params=_cparams("parallel", "parallel", "parallel"),
        name="banded_attn",
    )(*args)
    return [o.reshape(b, l, wq) for o in out]


def _banded_bias(tab, window, dist_scale):
    nb = -(-window // Q_BLOCK)
    kw = (nb + 1) * Q_BLOCK
    lv = kw + Q_BLOCK
    dist = nb * Q_BLOCK - _signed_index(lv, kw)
    v = _bias_of(tab, jnp.maximum(dist, 0) * dist_scale)
    return _toeplitz(v, Q_BLOCK, kw)


def _banded_gqa_kernel(*refs, hq, nb, window, has_sink, gate_row):
    it = iter(refs)
    q_ref = next(it)
    k_refs = [next(it) for _ in range(nb + 1)]
    vt_refs = [next(it) for _ in range(nb + 1)]
    bias_ref = next(it)
    sink_ref = next(it) if has_sink else None
    gate_ref = next(it) if gate_row is not None else None
    o_ref, qt_ref = next(it), next(it)
    hkv = 2
    g = hq // hkv
    pairs = g // 2
    kw = (nb + 1) * Q_BLOCK
    i = pl.program_id(1)
    k2 = jnp.concatenate([kr[...] for kr in k_refs], axis=0).astype(BF16)
    vt2 = jnp.concatenate([vr[...] for vr in vt_refs], axis=1).astype(BF16)
    c = lax.broadcasted_iota(jnp.int32, (kw, Q_BLOCK), 0)
    r = lax.broadcasted_iota(jnp.int32, (kw, Q_BLOCK), 1)
    dist = r + nb * Q_BLOCK - c
    ok = (dist >= 0) & (dist <= window) & ((i - nb) * Q_BLOCK + c >= 0)
    drop = jnp.where(ok, 0.0, NEG)
    for kh in range(hkv):
        q = q_ref[:, kh * g * HEAD_DIM:(kh + 1) * g * HEAD_DIM] * SCALE
        qt_ref[...] = jnp.zeros(qt_ref.shape, BF16)
        for pr in range(pairs):
            t = q[:, pr * LANES:(pr + 1) * LANES].T.astype(BF16)
            for half in range(2):
                h = 2 * pr + half
                qt_ref[kh * HEAD_DIM:(kh + 1) * HEAD_DIM, h * Q_BLOCK:(h + 1) * Q_BLOCK] = (
                    t[half * HEAD_DIM:(half + 1) * HEAD_DIM, :])
        st = _dot(k2, qt_ref[...]) + bias_ref[kh]
        vt = vt2[kh * HEAD_DIM:(kh + 1) * HEAD_DIM, :]
        if gate_ref is not None:
            gates = jax.nn.sigmoid(gate_ref[gate_row + kh * g:gate_row + (kh + 1) * g, :])
        outs = []
        for h in range(g):
            s_h = st[:, h * Q_BLOCK:(h + 1) * Q_BLOCK] + drop
            m = jnp.max(s_h, axis=0, keepdims=True)
            p = jnp.exp(s_h - m)
            l = jnp.sum(p, axis=0, keepdims=True)
            acc = _dot(vt, p.astype(BF16))
            if has_sink:
                sk = sink_ref[kh * g + h]
                big = jnp.maximum(m, sk)
                w = jnp.exp(m - big)
                o = (acc * w) / (l * w + jnp.exp(sk - big))
            else:
                o = acc / l
            if gate_ref is not None:
                o = o * gates[h:h + 1, :]
            outs.append(o)
        for pr in range(pairs):
            col = (kh * pairs + pr) * LANES
            o_ref[:, col:col + LANES] = jnp.concatenate(outs[2 * pr:2 * pr + 2], axis=0).T


def _banded_gqa(q, kv, vt, bias_t, *, hq, window, sink=None, gate_t=None, gate_row=None):
    b, l, wq = q.shape
    wk = 2 * HEAD_DIM
    g = hq // 2
    nq = l // Q_BLOCK
    nb = -(-window // Q_BLOCK)
    in_specs = [pl.BlockSpec((None, Q_BLOCK, wq), lambda bi, i: (bi, i, 0))]
    args = [q]
    for j in range(nb + 1):
        in_specs.append(pl.BlockSpec((None, Q_BLOCK, wk), lambda bi, i, j=j: (bi, jnp.maximum(i - nb + j, 0), 0)))
        args.append(kv)
    for j in range(nb + 1):
        in_specs.append(pl.BlockSpec((None, wk, Q_BLOCK), lambda bi, i, j=j: (bi, 0, jnp.maximum(i - nb + j, 0))))
        args.append(vt)
    in_specs.append(pl.BlockSpec(bias_t.shape, lambda bi, i: (0, 0, 0)))
    args.append(bias_t)
    if sink is not None:
        in_specs.append(pl.BlockSpec(memory_space=pltpu.SMEM))
        args.append(sink)
    if gate_t is not None:
        in_specs.append(pl.BlockSpec((None, gate_t.shape[1], Q_BLOCK), lambda bi, i: (bi, 0, i)))
        args.append(gate_t)
    return pl.pallas_call(
        functools.partial(_banded_gqa_kernel, hq=hq, nb=nb, window=window, has_sink=sink is not None,
                          gate_row=gate_row if gate_t is not None else None),
        grid=(b, nq),
        in_specs=in_specs,
        out_specs=pl.BlockSpec((None, Q_BLOCK, wq), lambda bi, i: (bi, i, 0)),
        out_shape=jax.ShapeDtypeStruct((b, l, wq), F32),
        scratch_shapes=[pltpu.VMEM((wk, g * Q_BLOCK), BF16)],
        compiler_params=_cparams("parallel", "parallel"),
        name="banded_gqa",
    )(*args)


def _banded_bias_t(tab, window):
    bias = _banded_bias(tab, window, 1)
    heads, _, kw = bias.shape
    g = heads // 2
    return bias.reshape(2, g, Q_BLOCK, kw).transpose(0, 3, 1, 2).reshape(2, kw, g * Q_BLOCK)


def _compress_finish_kernel(hab_ref, pe_ref, w2_ref, o_ref, *, nch, pad):
    hid_w = hab_ref.shape[1] // 2
    hab = hab_ref[...]
    ha = hab[:, :hid_w]
    hb_next = pltpu.roll(hab[:, hid_w:], nch - 1, axis=0)
    pe = pe_ref[0:1, :hid_w] + pe_ref[1:2, hid_w:]
    hid = _gelu_tanh(ha + hb_next + pe)
    ckv = _dot(hid.astype(BF16), w2_ref[...])
    if pad:
        o_ref[0:pad, :] = jnp.zeros((pad, o_ref.shape[1]), F32)
        o_ref[pad + nch:, :] = jnp.zeros((o_ref.shape[0] - pad - nch, o_ref.shape[1]), F32)
    o_ref[pad:pad + nch, :] = ckv


def _compress_finish(hab, pe_rows, w2big, layer, *, pad, rows_out):
    b, nch, w = hab.shape
    n_out = w2big.shape[2]
    return pl.pallas_call(
        functools.partial(_compress_finish_kernel, nch=nch, pad=pad),
        grid=(b,),
        in_specs=[pl.BlockSpec((None, nch, w), lambda bi: (bi, 0, 0)),
                  pl.BlockSpec(pe_rows.shape, lambda bi: (0, 0)),
                  pl.BlockSpec((None,) + w2big.shape[1:], lambda bi: (layer, 0, 0))],
        out_specs=pl.BlockSpec((None, rows_out, n_out), lambda bi: (bi, 0, 0)),
        out_shape=jax.ShapeDtypeStruct((b, rows_out, n_out), F32),
        compiler_params=_cparams("parallel"),
        name="compress_finish",
    )(hab, pe_rows, w2big)


def _compress_weights(cmp_w1, cmp_w2):
    lyr = cmp_w1.shape[0]
    eye_h = jnp.eye(NSA_KV_HEADS, dtype=F32)
    eye_p = jnp.eye(2, dtype=F32)
    w1 = cmp_w1.reshape(lyr, 2, 2, CMP_STRIDE, HEAD_DIM, CMP_HID)
    w1big = jnp.einsum("lpasde,pq,hk->lsphdaqke", w1, eye_p, eye_h)
    w1big = w1big.reshape(lyr, CMP_STRIDE * 2 * NSA_KV_HEADS * HEAD_DIM, 2 * 2 * NSA_KV_HEADS * CMP_HID)
    w2big = jnp.einsum("lped,pq,hk->lpheqkd", cmp_w2, eye_p, eye_h)
    w2big = w2big.reshape(lyr, 2 * NSA_KV_HEADS * CMP_HID, 2 * NSA_KV_HEADS * HEAD_DIM)
    w1p = jnp.einsum("lpasde,hk->lsphdake", w1, eye_h)
    w1p = w1p.reshape(lyr, CMP_STRIDE, 2, NSA_KV_HEADS * HEAD_DIM, 2 * NSA_KV_HEADS * CMP_HID)
    return w1big.astype(BF16), w2big.astype(BF16), w1p.astype(BF16)


def _pe_rows(cmp_pe):
    lyr = cmp_pe.shape[0]
    pe = cmp_pe.reshape(lyr, 2, 2, CMP_STRIDE, HEAD_DIM)
    pe = jnp.broadcast_to(pe[:, :, :, :, None, :], (lyr, 2, 2, CMP_STRIDE, NSA_KV_HEADS, HEAD_DIM))
    pe = pe.transpose(0, 2, 3, 1, 4, 5).reshape(lyr, 2, CMP_STRIDE * 2 * NSA_KV_HEADS * HEAD_DIM)
    return jnp.pad(pe, ((0, 0), (0, 6), (0, 0)))


def _overlap_table(nch, n_slc_pad):
    c = jnp.arange(nch)[:, None]
    jb = jnp.arange(n_slc_pad)[None, :]
    cstart = c * CMP_STRIDE
    cend = cstart + CMP_LEN - 1
    ov = (cstart <= jb * SLC_BLOCK + SLC_BLOCK - 1) & (cend >= jb * SLC_BLOCK) & (c < nch - 1)
    return ov.astype(BF16)


def _topk_select(imp, k_sel):
    lane = lax.broadcasted_iota(jnp.int32, imp.shape, 1).astype(F32)
    sel = jnp.zeros(imp.shape, F32)
    work = imp
    picks = []
    for _ in range(k_sel):
        mx = jnp.max(work, axis=-1, keepdims=True)
        idx = jnp.min(jnp.where(work == mx, lane, float(imp.shape[1])), axis=-1, keepdims=True)
        pick = lane == idx
        sel = jnp.where(pick, 1.0, sel)
        work = jnp.where(pick, REMOVED, work)
        picks.append(idx)
    return sel, picks


def _cmp_prompt_kernel(q_ref, ckv_ref, bias_ref, ov_ref, gate_ref, o_ref, imp_ref, qh_ref, ql_ref, *, nch, n_slc):
    i = pl.program_id(1)
    g = NSA_HEADS // NSA_KV_HEADS
    wk = NSA_KV_HEADS * HEAD_DIM
    pairs = g // 2
    cpq = Q_BLOCK // CMP_STRIDE
    start = pl.multiple_of(i * cpq, 8)
    win = ckv_ref[pl.ds(start, nch), :]
    ov = ov_ref[pl.ds(start, nch), :]
    kh_hi, kh_lo = _split_bf16(win[:, :wk])
    vt2 = win[:, wk:].T.astype(BF16)
    w = lax.broadcasted_iota(jnp.int32, (nch, Q_BLOCK), 0)
    r = lax.broadcasted_iota(jnp.int32, (nch, Q_BLOCK), 1)
    e = w - (nch - cpq)
    ok = (r - CMP_STRIDE * e - (CMP_LEN - 1) >= 0) & (e + i * cpq >= 0)
    keep = jnp.where(ok, 1.0, 0.0)
    drop = (keep - 1.0) * (-NEG)

    n_slc_pad = ov_ref.shape[1]
    jb = lax.broadcasted_iota(jnp.int32, (Q_BLOCK, n_slc_pad), 1)
    qpos = i * Q_BLOCK + lax.broadcasted_iota(jnp.int32, (Q_BLOCK, n_slc_pad), 0)
    cur = qpos // SLC_BLOCK
    valid = (jb * SLC_BLOCK <= qpos) & (jb < n_slc)
    forced = (jb == 0) | (jb == cur) | (jb == cur - 1)

    for kh in range(NSA_KV_HEADS):
        q = q_ref[:, kh * g * HEAD_DIM:(kh + 1) * g * HEAD_DIM] * SCALE
        qh_ref[...] = jnp.zeros(qh_ref.shape, BF16)
        ql_ref[...] = jnp.zeros(ql_ref.shape, BF16)
        for pr in range(pairs):
            t_hi, t_lo = _split_bf16(q[:, pr * LANES:(pr + 1) * LANES].T)
            for half in range(2):
                h = 2 * pr + half
                rows = slice(kh * HEAD_DIM, (kh + 1) * HEAD_DIM)
                src = slice(half * HEAD_DIM, (half + 1) * HEAD_DIM)
                qh_ref[rows, h * Q_BLOCK:(h + 1) * Q_BLOCK] = t_hi[src, :]
                ql_ref[rows, h * Q_BLOCK:(h + 1) * Q_BLOCK] = t_lo[src, :]
        st = _dot(kh_hi, qh_ref[...]) + (_dot(kh_hi, ql_ref[...]) + _dot(kh_lo, qh_ref[...]))
        st = st + bias_ref[kh]
        vt = vt2[kh * HEAD_DIM:(kh + 1) * HEAD_DIM, :]
        gates = jax.nn.sigmoid(gate_ref[kh * g:(kh + 1) * g, :])
        psum_t = jnp.zeros((nch, Q_BLOCK), F32)
        outs = []
        for h in range(g):
            s_h = st[:, h * Q_BLOCK:(h + 1) * Q_BLOCK] + drop
            m = jnp.max(s_h, axis=0, keepdims=True)
            p = jnp.exp(s_h - m) * keep
            p = p / jnp.maximum(jnp.sum(p, axis=0, keepdims=True), 1e-30)
            outs.append(_dot(vt, p.astype(BF16)) * gates[h:h + 1, :])
            psum_t = psum_t + p
        for pr in range(pairs):
            col = (kh * pairs + pr) * LANES
            o_ref[:, col:col + LANES] = jnp.concatenate(outs[2 * pr:2 * pr + 2], axis=0).T
        imp = _dot_2pass_lhs(psum_t.T, ov)
        imp_ref[kh] = jnp.where(valid, imp + jnp.where(forced, FORCE, 0.0), NEG)


def _cmp_prompt(q, ckv_pad, bias_rel_t, ov_pad, gate_t, *, nch, n_slc):
    b, l, wq = q.shape
    nq = l // Q_BLOCK
    n_slc_pad = ov_pad.shape[1]
    ql = (NSA_HEADS // NSA_KV_HEADS) * Q_BLOCK
    wk = NSA_KV_HEADS * HEAD_DIM
    return pl.pallas_call(
        functools.partial(_cmp_prompt_kernel, nch=nch, n_slc=n_slc),
        grid=(b, nq),
        in_specs=[pl.BlockSpec((None, Q_BLOCK, wq), lambda bi, i: (bi, i, 0)),
                  pl.BlockSpec((None,) + ckv_pad.shape[1:], lambda bi, i: (bi, 0, 0)),
                  pl.BlockSpec(bias_rel_t.shape, lambda bi, i: (0, 0, 0)),
                  pl.BlockSpec(ov_pad.shape, lambda bi, i: (0, 0)),
                  pl.BlockSpec((None, NSA_HEADS, Q_BLOCK), lambda bi, i: (bi, 0, i))],
        out_specs=[pl.BlockSpec((None, Q_BLOCK, wq), lambda bi, i: (bi, i, 0)),
                   pl.BlockSpec((None, NSA_KV_HEADS, Q_BLOCK, n_slc_pad), lambda bi, i: (bi, 0, i, 0))],
        out_shape=[jax.ShapeDtypeStruct((b, l, wq), F32),
                   jax.ShapeDtypeStruct((b, NSA_KV_HEADS, l, n_slc_pad), F32)],
        scratch_shapes=[pltpu.VMEM((wk, ql), BF16), pltpu.VMEM((wk, ql), BF16)],
        compiler_params=_cparams("parallel", "parallel"),
        name="cmp_prompt",
    )(q, ckv_pad, bias_rel_t, ov_pad, gate_t)


def _cmp_rel_bias_t(tab, nch):
    heads = tab.shape[1]
    g = NSA_HEADS // NSA_KV_HEADS
    cpq = Q_BLOCK // CMP_STRIDE
    lv = nch + cpq
    u = _signed_index(lv, nch)
    dist = CMP_STRIDE * ((nch - cpq) - u[None, :]) + jnp.arange(CMP_STRIDE)[:, None] - (CMP_LEN - 1)
    t = _toeplitz(_bias_of(tab, jnp.maximum(dist, 0)), cpq, nch)
    t = t.transpose(0, 2, 1, 3).reshape(NSA_KV_HEADS, g, Q_BLOCK, nch)
    return t.transpose(0, 3, 1, 2).reshape(NSA_KV_HEADS, nch, g * Q_BLOCK)


def _topk_rows_kernel(imp_ref, sel_ref, *, k_sel):
    sel, _ = _topk_select(imp_ref[...], k_sel)
    sel_ref[...] = sel.astype(BF16)


def _topk_rows(imp, k_sel, *, tr):
    rows, c = imp.shape
    assert rows % tr == 0
    return pl.pallas_call(
        functools.partial(_topk_rows_kernel, k_sel=k_sel),
        grid=(rows // tr,),
        in_specs=[pl.BlockSpec((tr, c), lambda i: (i, 0))],
        out_specs=pl.BlockSpec((tr, c), lambda i: (i, 0)),
        out_shape=jax.ShapeDtypeStruct((rows, c), BF16),
        compiler_params=_cparams("parallel"),
        name="topk_rows",
    )(imp)


def _slc_prompt_kernel(q_ref, k_ref, vt_ref, sel_ref, bias_ref, gate_ref, o_ref,
                       qt_ref, m_ref, l_ref, acc_ref, *, nd):
    kvh = pl.program_id(1)
    i = pl.program_id(2)
    g = NSA_HEADS // NSA_KV_HEADS
    ql = g * Q_BLOCK
    pairs = g // 2
    q = q_ref[...] * SCALE
    qt_ref[...] = jnp.zeros(qt_ref.shape, BF16)
    row0 = pl.multiple_of(kvh * HEAD_DIM, HEAD_DIM)
    for pr in range(pairs):
        t = q[:, pr * LANES:(pr + 1) * LANES].T
        for half in range(2):
            h = 2 * pr + half
            qt_ref[pl.ds(row0, HEAD_DIM), h * Q_BLOCK:(h + 1) * Q_BLOCK] = (
                t[half * HEAD_DIM:(half + 1) * HEAD_DIM, :].astype(BF16))
    m_ref[...] = jnp.full(m_ref.shape, NEG, F32)
    l_ref[...] = jnp.zeros(l_ref.shape, F32)
    acc_ref[...] = jnp.zeros(acc_ref.shape, F32)

    sel = sel_ref[...]
    n_slc_pad = sel.shape[1]
    key_blk = lax.broadcasted_iota(jnp.int32, (Q_BLOCK, n_slc_pad), 0) // SLC_BLOCK
    blk_id = lax.broadcasted_iota(jnp.int32, (Q_BLOCK, n_slc_pad), 1)
    kk = lax.broadcasted_iota(jnp.int32, (Q_BLOCK, Q_BLOCK), 0)
    qq = lax.broadcasted_iota(jnp.int32, (Q_BLOCK, Q_BLOCK), 1)
    bpt = Q_BLOCK // SLC_BLOCK

    def body(jj, carry):
        k_parts, v_parts, b_parts, d_parts = [], [], [], []
        for sub in range(SLC_TILES_PER_STEP):
            j = jj * SLC_TILES_PER_STEP + sub
            jc = jnp.minimum(j, i)
            col0 = pl.multiple_of(jc * Q_BLOCK, Q_BLOCK)
            v_parts.append(vt_ref[pl.ds(row0, HEAD_DIM), pl.ds(col0, Q_BLOCK)].astype(BF16))
            b_parts.append(bias_ref[jnp.minimum(i - jc, nd)])
            expand = jnp.where(blk_id == jc * bpt + key_blk, 1.0, 0.0).astype(BF16)
            picked = _dot_nt(expand, sel) > 0.5
            d_parts.append(jnp.where(picked & ((kk <= qq) | (j < i)) & (j <= i), 0.0, NEG))
            k_parts.append(k_ref[pl.ds(col0, Q_BLOCK), :].astype(BF16))
        drop = jnp.concatenate(d_parts, axis=0)
        st = _dot(jnp.concatenate(k_parts, axis=0), qt_ref[...]) + jnp.concatenate(b_parts, axis=0)
        vt = jnp.concatenate(v_parts, axis=1)
        for h in range(g):
            hs = slice(h * Q_BLOCK, (h + 1) * Q_BLOCK)
            s_h = st[:, hs] + drop
            m_prev = m_ref[:, hs]
            m_new = jnp.maximum(m_prev, jnp.max(s_h, axis=0, keepdims=True))
            alpha = jnp.exp(m_prev - m_new)
            p = jnp.exp(s_h - m_new)
            l_ref[:, hs] = alpha * l_ref[:, hs] + jnp.sum(p, axis=0, keepdims=True)
            acc_ref[:, hs] = alpha * acc_ref[:, hs] + _dot(vt, p.astype(BF16))
            m_ref[:, hs] = m_new
        return carry

    lax.fori_loop(0, (i + SLC_TILES_PER_STEP) // SLC_TILES_PER_STEP, body, 0)
    ot = acc_ref[...] / l_ref[...]
    grow0 = pl.multiple_of(NSA_HEADS + kvh * g, g)
    gates = jax.nn.sigmoid(gate_ref[pl.ds(grow0, g), :])
    for pr in range(pairs):
        parts = []
        for half in range(2):
            h = 2 * pr + half
            parts.append(ot[:, h * Q_BLOCK:(h + 1) * Q_BLOCK] * gates[h:h + 1, :])
        o_ref[:, pr * LANES:(pr + 1) * LANES] = jnp.concatenate(parts, axis=0).T


def _slc_prompt(q, kv_slc, vt_slc, sel, bias_tiles, gate_t):
    b, l, wq = q.shape
    nq = l // Q_BLOCK
    g = NSA_HEADS // NSA_KV_HEADS
    wg = g * HEAD_DIM
    wk = NSA_KV_HEADS * HEAD_DIM
    nd = bias_tiles.shape[0] - 1
    n_slc_pad = sel.shape[3]
    ql = g * Q_BLOCK
    return pl.pallas_call(
        functools.partial(_slc_prompt_kernel, nd=nd),
        grid=(b, NSA_KV_HEADS, nq),
        in_specs=[pl.BlockSpec((None, Q_BLOCK, wg), lambda bi, kh, i: (bi, i, kh)),
                  pl.BlockSpec((None, l, wk), lambda bi, kh, i: (bi, 0, 0)),
                  pl.BlockSpec((None, wk, l), lambda bi, kh, i: (bi, 0, 0)),
                  pl.BlockSpec((None, None, Q_BLOCK, n_slc_pad), lambda bi, kh, i: (bi, kh, i, 0)),
                  pl.BlockSpec((nd + 1, None, Q_BLOCK, ql), lambda bi, kh, i: (0, kh, 0, 0)),
                  pl.BlockSpec((None, gate_t.shape[1], Q_BLOCK), lambda bi, kh, i: (bi, 0, i))],
        out_specs=pl.BlockSpec((None, Q_BLOCK, wg), lambda bi, kh, i: (bi, i, kh)),
        out_shape=jax.ShapeDtypeStruct((b, l, wq), F32),
        scratch_shapes=[pltpu.VMEM((wk, ql), BF16), pltpu.VMEM((1, ql), F32), pltpu.VMEM((1, ql), F32),
                        pltpu.VMEM((HEAD_DIM, ql), F32)],
        compiler_params=_cparams("parallel", "parallel", "arbitrary"),
        name="slc_prompt",
    )(q, kv_slc, vt_slc, sel, bias_tiles, gate_t)


def _slc_bias_tiles(tab, nq):
    nd = min(nq, 13)
    heads = tab.shape[1]
    g = NSA_HEADS // NSA_KV_HEADS
    t = _bias_of(tab, jnp.arange((nd + 1) * Q_BLOCK))
    rel = _signed_index(2 * Q_BLOCK, Q_BLOCK)
    idx = jnp.maximum(jnp.arange(nd)[:, None] * Q_BLOCK + rel[None, :], 0)
    near = _toeplitz(t[:, idx], Q_BLOCK, Q_BLOCK)
    far = jnp.broadcast_to(tab[RP_BUCKETS - 1].astype(F32)[:, None, None, None], (heads, 1, Q_BLOCK, Q_BLOCK))
    tiles = jnp.concatenate([near, far], axis=1)
    tiles = tiles.reshape(NSA_KV_HEADS, g, nd + 1, Q_BLOCK, Q_BLOCK).transpose(2, 0, 3, 1, 4)
    return tiles.reshape(nd + 1, NSA_KV_HEADS, Q_BLOCK, g * Q_BLOCK)


def _post_kernel(*refs, mode, n_attn, ff_chunk):
    it = iter(refs)
    x_ref = next(it)
    if mode == "even":
        oa_ref = next(it)
        stats = [(next(it), next(it), next(it)) for _ in range((n_attn - 1) // 3)]
    else:
        branch_refs = [next(it) for _ in range(n_attn)]
    wout_ref, g_ref, gt1_ref, sc2_ref, sh2_ref, gt2_ref, wup_ref, wdown_ref, o_ref = it

    if mode == "even":
        ms = [m_ref[...] for _, m_ref, _ in stats]
        big = functools.reduce(jnp.maximum, ms)
        l_tot = 0.0
        acc_tot = 0.0
        for (acc_ref, _, l_ref), m in zip(stats, ms):
            w = jnp.exp(m - big)
            l_tot = l_tot + l_ref[...] * w
            acc_tot = acc_tot + acc_ref[...] * w
        ob = acc_tot / l_tot
        o = jnp.concatenate([oa_ref[...], ob], axis=1)
    else:
        o = branch_refs[0][...]
        for br in branch_refs[1:]:
            o = o + br[...]

    y = _dot(o.astype(BF16), wout_ref[...])
    x1 = x_ref[...] + gt1_ref[...] * _rms(y, g_ref[1:2, :])
    h2 = (_rms(x1, g_ref[2:3, :]) * (1.0 + sc2_ref[...]) + sh2_ref[...]).astype(BF16)
    d_ff = wup_ref.shape[1]
    u = jnp.zeros(x1.shape, F32)
    for c0 in range(0, d_ff, ff_chunk):
        a = jnp.square(jnp.maximum(_dot(h2, wup_ref[:, c0:c0 + ff_chunk]), 0.0))
        u = u + _dot(a.astype(BF16), wdown_ref[c0:c0 + ff_chunk, :])
    o_ref[...] = x1 + gt2_ref[...] * _rms(u, g_ref[3:4, :])


def _post_block(x, attn, w_out, norm_g, mods, w_up, w_down, layer, *, mode, tm):
    b, l, d = x.shape
    r = mods[0].shape[1]
    rb = 1 if r == 1 else tm
    mod_map = (lambda bi, i: (bi, 0, 0)) if r == 1 else (lambda bi, i: (bi, i, 0))
    d_ff = w_up.shape[2]
    tok = lambda bi, i: (bi, i, 0)
    in_specs = [pl.BlockSpec((None, tm, d), tok)]
    in_specs += [pl.BlockSpec((None, tm, a.shape[2]), tok) for a in attn]
    in_specs += [pl.BlockSpec((None, d, d), lambda bi, i: (layer // 2, 0, 0)),
                 pl.BlockSpec((None, 4, d), lambda bi, i: (layer, 0, 0))]
    in_specs += [pl.BlockSpec((None, rb, d), mod_map)] * 4
    in_specs += [pl.BlockSpec((None, d, d_ff), lambda bi, i: (layer, 0, 0)),
                 pl.BlockSpec((None, d_ff, d), lambda bi, i: (layer, 0, 0))]
    return pl.pallas_call(
        functools.partial(_post_kernel, mode=mode, n_attn=len(attn), ff_chunk=1024),
        grid=(b, l // tm),
        in_specs=in_specs,
        out_specs=pl.BlockSpec((None, tm, d), tok),
        out_shape=jax.ShapeDtypeStruct((b, l, d), F32),
        compiler_params=_cparams("parallel", "parallel"),
        name="post_block",
    )(x, *attn, w_out, norm_g, *mods, w_up, w_down)


def _decode_cache_kernel(*refs, hq, hkv, mult_new, has_sink, has_gate, has_prev):
    it = iter(refs)
    q_ref, new_ref, c_ref, bias_ref, biasn_ref, mult_ref = [next(it) for _ in range(6)]
    sink_ref = next(it) if has_sink else None
    gate_ref = next(it) if has_gate else None
    if has_prev:
        next(it)
    o_ref, cout_ref = next(it), next(it)
    g = hq // hkv
    r = c_ref.shape[1]
    hd = HEAD_DIM
    s_rows, s_new = [], []
    for kh in range(hkv):
        kt = c_ref[kh * hd:(kh + 1) * hd, :]
        kn = new_ref[kh * hd:(kh + 1) * hd, :]
        for gi in range(g):
            h = kh * g + gi
            qc = q_ref[h * hd:(h + 1) * hd, :] * SCALE
            s_rows.append(jnp.sum(kt * qc, axis=0, keepdims=True))
            s_new.append(jnp.sum(kn * qc, axis=0, keepdims=True))
    mult = mult_ref[...]
    s = jnp.where(mult > 0.0, jnp.concatenate(s_rows, axis=0) + bias_ref[...], NEG)
    sn = jnp.concatenate(s_new, axis=0) + biasn_ref[...]
    m = jnp.maximum(jnp.max(s, axis=1, keepdims=True), sn)
    if has_sink:
        m = jnp.maximum(m, sink_ref[...])
    p = mult * jnp.exp(s - m)
    pn = mult_new * jnp.exp(sn - m)
    l = jnp.sum(p, axis=1, keepdims=True) + pn
    if has_sink:
        l = l + jnp.exp(sink_ref[...] - m)
    inv = 1.0 / l
    if has_gate:
        inv = inv * jax.nn.sigmoid(gate_ref[...])
    for kh in range(hkv):
        vt = c_ref[(hkv + kh) * hd:(hkv + kh + 1) * hd, :]
        vn = new_ref[(hkv + kh) * hd:(hkv + kh + 1) * hd, :]
        for gi in range(g):
            h = kh * g + gi
            o = jnp.sum(vt * p[h:h + 1, :], axis=1, keepdims=True) + vn * pn[h:h + 1, :]
            o_ref[h * hd:(h + 1) * hd, :] = o * inv[h:h + 1, :]
    lane = lax.broadcasted_iota(jnp.int32, (hd, r), 1)
    for c in range(2 * hkv):
        rows = slice(c * hd, (c + 1) * hd)
        shifted = pltpu.roll(c_ref[rows, :], r - 1, axis=1)
        cout_ref[rows, :] = jnp.where(lane == r - 1, new_ref[rows, :], shifted)


def _cache_view(cache):
    lyr, n, r = cache.shape[:3]
    return cache.transpose(0, 1, 3, 4, 5, 2).reshape(lyr, n, -1, r)


def _cache_unview(view, hkv):
    lyr, n, c, r = view.shape
    return view.reshape(lyr, n, 2, hkv, HEAD_DIM, r).transpose(0, 1, 5, 2, 3, 4)


def _decode_cache(q, new, view, layer, tab, patterns, *, hq, hkv, prev, sink=None, gate=None):
    lyr, n, c, wb = view.shape
    dist = wb - jnp.arange(wb)
    mult = sum(((dist <= w) & (dist % d == 0)).astype(F32) for w, d in patterns)[None, :]
    consts = [_bias_of(tab, dist), _bias_of(tab, jnp.zeros((1,), jnp.int32)), mult]
    if sink is not None:
        consts.append(sink.reshape(hq, 1).astype(F32))
    in_specs = [pl.BlockSpec((None, hq * HEAD_DIM, 1), lambda i: (i, 0, 0)),
                pl.BlockSpec((None, c, 1), lambda i: (i, 0, 0)),
                pl.BlockSpec((None, None, c, wb), lambda i: (layer, i, 0, 0))]
    in_specs += [pl.BlockSpec(cst.shape, lambda i: (0, 0)) for cst in consts]
    args = [q.reshape(n, hq * HEAD_DIM, 1), new.reshape(n, c, 1), view] + consts
    if gate is not None:
        in_specs.append(pl.BlockSpec((None, hq, 1), lambda i: (i, 0, 0)))
        args.append(gate.reshape(n, hq, 1))
    in_specs.append(pl.BlockSpec(memory_space=pl.ANY))
    aliases = {len(args): 1}
    args.append(prev)
    o, buf = pl.pallas_call(
        functools.partial(_decode_cache_kernel, hq=hq, hkv=hkv, mult_new=float(len(patterns)),
                          has_sink=sink is not None, has_gate=gate is not None, has_prev=True),
        grid=(n,),
        in_specs=in_specs,
        out_specs=[pl.BlockSpec((None, hq * HEAD_DIM, 1), lambda i: (i, 0, 0)),
                   pl.BlockSpec((None, None, c, wb), lambda i: (layer, i, 0, 0))],
        out_shape=[jax.ShapeDtypeStruct((n, hq * HEAD_DIM, 1), F32),
                   jax.ShapeDtypeStruct((lyr, n, c, wb), F32)],
        input_output_aliases=aliases,
        compiler_params=_cparams("arbitrary"),
        name="decode_cache",
    )(*args)
    return o.reshape(n, hq * HEAD_DIM), buf


def _paged_hab_kernel(pt_ref, *refs, n_pages):
    page_refs = refs[:n_pages]
    perm_ref, w_ref, o_ref, z_ref = refs[n_pages:]
    page = page_refs[0].shape[1]
    cpp = page // CMP_STRIDE
    hw = w_ref.shape[3] // 2
    perm = perm_ref[...]
    for pg in range(n_pages):
        xp = _dot_nt(perm, page_refs[pg][...].astype(BF16))
        for s in range(CMP_STRIDE):
            z_ref[s, pg * cpp:(pg + 1) * cpp, :] = xp[s * cpp:(s + 1) * cpp, :]
    for p in range(2):
        acc = None
        for s in range(CMP_STRIDE):
            part = _dot(z_ref[s, :, p * LANES:(p + 1) * LANES].astype(BF16), w_ref[s, p])
            acc = part if acc is None else acc + part
        o_ref[:, p * hw:(p + 1) * hw] = acc[:, :hw]
        o_ref[:, (2 + p) * hw:(3 + p) * hw] = acc[:, hw:]


def _paged_hab(pool_view, page_table, w1p, layer):
    lyr, n_pool, c, page = pool_view.shape
    n, n_pages = page_table.shape
    cpp = page // CMP_STRIDE
    nch = n_pages * cpp
    nout = 2 * w1p.shape[4]
    rows = jnp.arange(page)
    perm = (rows[None, :] == (rows[:, None] % cpp) * CMP_STRIDE + rows[:, None] // cpp).astype(BF16)

    def page_map(k):
        return lambda i, pt: (layer, pt[i, k], 0, 0)

    grid_spec = pltpu.PrefetchScalarGridSpec(
        num_scalar_prefetch=1,
        grid=(n,),
        in_specs=[pl.BlockSpec((None, None, c, page), page_map(k)) for k in range(n_pages)]
        + [pl.BlockSpec(perm.shape, lambda i, pt: (0, 0)),
           pl.BlockSpec((None,) + w1p.shape[1:], lambda i, pt: (layer, 0, 0, 0, 0))],
        out_specs=pl.BlockSpec((None, nch, nout), lambda i, pt: (i, 0, 0)),
        scratch_shapes=[pltpu.VMEM((CMP_STRIDE, nch, c), F32)],
    )
    return pl.pallas_call(
        functools.partial(_paged_hab_kernel, n_pages=n_pages),
        grid_spec=grid_spec,
        out_shape=jax.ShapeDtypeStruct((n, nch, nout), F32),
        compiler_params=_cparams("arbitrary"),
        name="paged_hab",
    )(page_table, *([pool_view] * n_pages), perm, w1p)


def _cmp_sample_kernel(q_ref, ckv_ref, bias_ref, ov_ref, gate_ref, o_ref, idx_ref, *, nch, n_slc, qpos):
    g = NSA_HEADS // NSA_KV_HEADS
    wk = NSA_KV_HEADS * HEAD_DIM
    nseq = q_ref.shape[0]
    c = lax.broadcasted_iota(jnp.int32, (g, nch), 1)
    mask = (c * CMP_STRIDE + CMP_LEN - 1 <= qpos) & (c < nch - 1)
    n_slc_pad = ov_ref.shape[1]
    jb = lax.broadcasted_iota(jnp.int32, (1, n_slc_pad), 1)
    cur = qpos // SLC_BLOCK
    valid = (jb * SLC_BLOCK <= qpos) & (jb < n_slc)
    forced = (jb == 0) | (jb == cur) | (jb == cur - 1)
    k_sel = min(N_SELECT, n_slc)
    imps = []
    for sq in range(nseq):
        ckv = ckv_ref[sq]
        q = q_ref[sq] * SCALE
        gates = jax.nn.sigmoid(gate_ref[sq])
        for kh in range(NSA_KV_HEADS):
            rs = slice(kh * g, (kh + 1) * g)
            k_h = ckv[:, kh * HEAD_DIM:(kh + 1) * HEAD_DIM]
            v_h = ckv[:, wk + kh * HEAD_DIM: wk + (kh + 1) * HEAD_DIM].astype(BF16)
            s = _dot_nt_3pass(q[rs, :], k_h) + bias_ref[rs, :]
            s = jnp.where(mask, s, NEG)
            m = jnp.max(s, axis=-1, keepdims=True)
            p = jnp.where(mask, jnp.exp(s - m), 0.0)
            p = p / jnp.maximum(jnp.sum(p, axis=-1, keepdims=True), 1e-30)
            o_ref[sq, rs, :] = _dot(p.astype(BF16), v_h) * gates[rs, 0:1]
            imp = jnp.sum(_dot_2pass_lhs(p, ov_ref[...]), axis=0, keepdims=True)
            imps.append(jnp.where(valid, imp + jnp.where(forced, FORCE, 0.0), NEG))
    _, picks = _topk_select(jnp.concatenate(imps, axis=0), k_sel)
    lane = lax.broadcasted_iota(jnp.int32, (nseq * NSA_KV_HEADS, LANES), 1)
    rows = jnp.zeros((nseq * NSA_KV_HEADS, LANES), jnp.int32)
    for t, idx in enumerate(picks):
        rows = jnp.where(lane == t, idx.astype(jnp.int32), rows)
    for sq in range(nseq):
        idx_ref[sq] = rows[sq * NSA_KV_HEADS:(sq + 1) * NSA_KV_HEADS, :]


def _cmp_sample(q, ckv, bias, ov, gate, *, n_slc, qpos):
    n = q.shape[0]
    nch = ckv.shape[1]
    nseq = math.gcd(n, CMP_SAMPLE_GROUP)
    return pl.pallas_call(
        functools.partial(_cmp_sample_kernel, nch=nch, n_slc=n_slc, qpos=qpos),
        grid=(n // nseq,),
        in_specs=[pl.BlockSpec((nseq, NSA_HEADS, HEAD_DIM), lambda i: (i, 0, 0)),
                  pl.BlockSpec((nseq, nch, ckv.shape[2]), lambda i: (i, 0, 0)),
                  pl.BlockSpec(bias.shape, lambda i: (0, 0)),
                  pl.BlockSpec(ov.shape, lambda i: (0, 0)),
                  pl.BlockSpec((nseq, NSA_HEADS, 3), lambda i: (i, 0, 0))],
        out_specs=[pl.BlockSpec((nseq, NSA_HEADS, HEAD_DIM), lambda i: (i, 0, 0)),
                   pl.BlockSpec((nseq, NSA_KV_HEADS, LANES), lambda i: (i, 0, 0))],
        out_shape=[jax.ShapeDtypeStruct((n, NSA_HEADS, HEAD_DIM), F32),
                   jax.ShapeDtypeStruct((n, NSA_KV_HEADS, LANES), jnp.int32)],
        compiler_params=_cparams("parallel"),
        name="cmp_sample",
    )(q, ckv, bias, ov, gate)


def _slc_sample_kernel(pg_ref, idx_ref, *refs, k_sel, nb_past):
    n_sel = NSA_KV_HEADS * k_sel
    page_refs = refs[:n_sel]
    q_ref, new_ref, bias_ref, biasn_ref, gate_ref, o_ref = refs[n_sel:]
    i = pl.program_id(0)
    g = NSA_HEADS // NSA_KV_HEADS
    wk = NSA_KV_HEADS * HEAD_DIM
    page = page_refs[0].shape[1]
    bpp = page // SLC_BLOCK
    lane_blk = lax.broadcasted_iota(jnp.int32, (g, page), 1) // SLC_BLOCK
    for kh in range(NSA_KV_HEADS):
        rs = slice(kh * g, (kh + 1) * g)
        ks = slice(kh * HEAD_DIM, (kh + 1) * HEAD_DIM)
        vs = slice(wk + kh * HEAD_DIM, wk + (kh + 1) * HEAD_DIM)
        qg = q_ref[rs, :] * SCALE
        sn = jnp.sum(qg * new_ref[:, ks], axis=-1, keepdims=True) + biasn_ref[rs, :]
        m = sn
        scores = []
        for t in range(k_sel):
            idx = idx_ref[(i * NSA_KV_HEADS + kh) * k_sel + t]
            lp = jnp.minimum(idx, nb_past - 1) // bpp
            kt = page_refs[kh * k_sel + t][ks, :].astype(BF16)
            s = _dot(qg.astype(BF16), kt) + bias_ref[lp, rs, :]
            ok = (lane_blk == idx % bpp) & (idx < nb_past)
            s = jnp.where(ok, s, NEG)
            m = jnp.maximum(m, jnp.max(s, axis=-1, keepdims=True))
            scores.append((s, ok))
        l = jnp.exp(sn - m)
        acc = l * new_ref[:, vs]
        for t in range(k_sel):
            s, ok = scores[t]
            p = jnp.where(ok, jnp.exp(s - m), 0.0)
            l = l + jnp.sum(p, axis=-1, keepdims=True)
            vt = page_refs[kh * k_sel + t][vs, :].astype(BF16)
            acc = acc + _dot_nt(p.astype(BF16), vt)
        o_ref[rs, :] = acc / l * jax.nn.sigmoid(gate_ref[rs, :])


def _slc_sample(q, new, pool_view, layer, pages, idx, bias, bias_new, gate, *, nb_past):
    n = q.shape[0]
    lyr, n_pool, c, page = pool_view.shape
    k_sel = pages.shape[0] // (n * NSA_KV_HEADS)
    n_sel = NSA_KV_HEADS * k_sel

    def page_map(k):
        return lambda i, pg, ids: (layer, pg[i * n_sel + k], 0, 0)

    grid_spec = pltpu.PrefetchScalarGridSpec(
        num_scalar_prefetch=2,
        grid=(n,),
        in_specs=[pl.BlockSpec((None, None, c, page), page_map(k)) for k in range(n_sel)]
        + [pl.BlockSpec((None, NSA_HEADS, HEAD_DIM), lambda i, pg, ids: (i, 0, 0)),
           pl.BlockSpec((None, 1, c), lambda i, pg, ids: (i, 0, 0)),
           pl.BlockSpec(bias.shape, lambda i, pg, ids: (0, 0, 0)),
           pl.BlockSpec(bias_new.shape, lambda i, pg, ids: (0, 0)),
           pl.BlockSpec((None, NSA_HEADS, 1), lambda i, pg, ids: (i, 0, 0))],
        out_specs=pl.BlockSpec((None, NSA_HEADS, HEAD_DIM), lambda i, pg, ids: (i, 0, 0)),
    )
    return pl.pallas_call(
        functools.partial(_slc_sample_kernel, k_sel=k_sel, nb_past=nb_past),
        grid_spec=grid_spec,
        out_shape=jax.ShapeDtypeStruct((n, NSA_HEADS, HEAD_DIM), F32),
        compiler_params=_cparams("arbitrary"),
        name="slc_sample",
    )(pages, idx, *([pool_view] * n_sel), q, new, bias, bias_new, gate)


EVEN_SPLITS = (A_HEADS * HEAD_DIM, 2 * A_KV_HEADS * HEAD_DIM, B_HEADS * HEAD_DIM, 2 * B_HEADS * HEAD_DIM)
NSA_KV_W = 2 * NSA_KV_HEADS * HEAD_DIM
NSA_SPLITS = (NSA_HEADS * HEAD_DIM, NSA_KV_W, NSA_KV_W, NSA_KV_W, LANES)
NSA_SPLITS_T = (NSA_KV_HEADS * HEAD_DIM, NSA_KV_HEADS * HEAD_DIM, 3 * NSA_HEADS)
EVEN_SPLITS_T = (A_KV_HEADS * HEAD_DIM,)


def _mod_parts(mod):
    if mod.ndim == 3:
        return [mod[:, i:i + 1] for i in range(6)]
    return [mod[:, :, i] for i in range(6)]


def _even_layer(xp, xs, mod_p, mod_s, view_swa, view_dil, bufs, li, layer, wts, tabs):
    w_in, w_in_t, w_out, sinks, norm_g, w_up, w_down = wts
    g0 = norm_g[layer, 0:1]
    sh1, sc1, gt1, sh2, sc2, gt2 = _mod_parts(mod_p)
    qa, kva, qb, kvb, vt_a = _norm_linear(xp, g0, sc1, sh1, w_in, li, EVEN_SPLITS, tm=512,
                                          w_t=w_in_t, splits_t=EVEN_SPLITS_T)
    oa = _banded_gqa(qa, kva, vt_a, tabs["swa_t"], hq=A_HEADS, window=WIN_A, sink=sinks[li])
    stats = []
    for (w, d), bias in zip(DIL_PAIRS, tabs["dil"]):
        stats += _banded_attn(qb, kvb, bias, hq=B_HEADS, hkv=B_HEADS, window=w // d, dil=d, mode="stats")
    xp_new = _post_block(xp, [oa] + stats, w_out, norm_g, (gt1, sc2, sh2, gt2), w_up, w_down, layer,
                         mode="even", tm=256)
    l = xp.shape[1]
    state_p = (kva[:, l - min(WIN_A, l):], kvb[:, l - min(DIL_MAX_WIN, l):])
    sh1, sc1, gt1, sh2, sc2, gt2 = _mod_parts(mod_s)
    n = xs.shape[1]
    qa, kva, qb, kvb = [a[0] for a in _norm_linear(xs, g0, sc1, sh1, w_in, li, EVEN_SPLITS, tm=n)]
    oa, bufs["swa"] = _decode_cache(qa, kva, view_swa, li, tabs["rel"][:, :A_HEADS], ((WIN_A, 1),),
                                    hq=A_HEADS, hkv=A_KV_HEADS, prev=bufs["swa"], sink=sinks[li])
    ob, bufs["dil"] = _decode_cache(qb, kvb, view_dil, li, tabs["rel"][:, A_HEADS:], DIL_PAIRS,
                                    hq=B_HEADS, hkv=B_HEADS, prev=bufs["dil"])
    o = jnp.concatenate([oa, ob], axis=1)[None]
    xs_new = _post_block(xs, [o], w_out, norm_g, (gt1, sc2, sh2, gt2), w_up, w_down, layer,
                         mode="sum", tm=n)
    return xp_new, xs_new, state_p


def _nsa_layer(xp, xs, mod_p, mod_s, view_cmp, view_slc, view_win, bufs, page_table, li, layer, wts, tabs):
    w_in, w_in_t, w_out, w1big, w2big, w1p, pe_rows, norm_g, w_up, w_down = wts
    g0 = norm_g[layer, 0:1]
    b, l, _ = xp.shape
    pe_h = _linear(pe_rows[li], w1big, li, tm=8, tn=512)
    sh1, sc1, gt1, sh2, sc2, gt2 = _mod_parts(mod_p)
    q, kv_cmp, kv_slc, kv_win, _, vt_slc, vt_win, gate_t = _norm_linear(
        xp, g0, sc1, sh1, w_in, li, NSA_SPLITS, tm=512, w_t=w_in_t, splits_t=NSA_SPLITS_T)
    nch = l // CMP_STRIDE
    n_slc = -(-l // SLC_BLOCK)
    hab = _linear(kv_cmp.reshape(b * nch, CMP_STRIDE * NSA_KV_W), w1big, li, tm=min(512, b * nch), tn=512)
    pad = nch - Q_BLOCK // CMP_STRIDE
    ckv_pad = _compress_finish(hab.reshape(b, nch, -1), pe_h, w2big, li, pad=pad, rows_out=2 * nch)
    o_cmp, imp = _cmp_prompt(q, ckv_pad, tabs["cmp_rel_t"], tabs["ov_pad"], gate_t, nch=nch, n_slc=n_slc)
    n_rows = b * NSA_KV_HEADS * l
    sel = _topk_rows(imp.reshape(n_rows, -1), min(N_SELECT, n_slc), tr=min(1024, n_rows)).reshape(imp.shape)
    o_slc = _slc_prompt(q, kv_slc, vt_slc, sel, tabs["slc_tiles"], gate_t)
    o_win = _banded_gqa(q, kv_win, vt_win, tabs["win_t"], hq=NSA_HEADS, window=WIN_C,
                        gate_t=gate_t, gate_row=2 * NSA_HEADS)
    xp_new = _post_block(xp, [o_cmp, o_slc, o_win], w_out, norm_g, (gt1, sc2, sh2, gt2), w_up, w_down,
                         layer, mode="sum", tm=256)
    state_p = (kv_cmp, kv_slc, kv_win[:, l - min(WIN_C, l):])
    sh1, sc1, gt1, sh2, sc2, gt2 = _mod_parts(mod_s)
    n = xs.shape[1]
    q, kv_cmp_s, kv_slc_s, kv_win_s, gate = [a[0] for a in
                                             _norm_linear(xs, g0, sc1, sh1, w_in, li, NSA_SPLITS, tm=n)]
    n_pages = page_table.shape[1]
    page = view_cmp.shape[3]
    past = n_pages * page
    assert past % SLC_BLOCK == 0 and past % CMP_STRIDE == 0
    nch_s = past // CMP_STRIDE
    nb_past = past // SLC_BLOCK
    n_slc_s = nb_past + 1
    hab_s = _paged_hab(view_cmp, page_table, w1p, li)
    ckv_s = _compress_finish(hab_s, pe_h, w2big, li, pad=0, rows_out=nch_s)
    gate3 = gate[:, :NSA_HEADS * 3].reshape(n, NSA_HEADS, 3)
    o_cmp, idx = _cmp_sample(q.reshape(n, NSA_HEADS, HEAD_DIM), ckv_s, tabs["cmp_s"], tabs["ov_s"], gate3,
                             n_slc=n_slc_s, qpos=past)
    k_sel = min(N_SELECT, n_slc_s)
    idx = idx[:, :, :k_sel]
    bpp = page // SLC_BLOCK
    pidx = jnp.clip(idx, 0, nb_past - 1)
    phys = jnp.take_along_axis(page_table, (pidx // bpp).reshape(n, -1), axis=1)
    o_slc = _slc_sample(q.reshape(n, NSA_HEADS, HEAD_DIM), kv_slc_s[:, None], view_slc, li, phys.reshape(-1),
                        idx.reshape(-1), tabs["slc_s"], tabs["slc_s_new"], gate3[:, :, 1:2], nb_past=nb_past)
    o_win, bufs["win"] = _decode_cache(q, kv_win_s, view_win, li, tabs["rel"], ((WIN_C, 1),), hq=NSA_HEADS,
                                       hkv=NSA_KV_HEADS, prev=bufs["win"], gate=gate3[:, :, 2])
    attn = [o_cmp.reshape(1, n, -1), o_slc.reshape(1, n, -1), o_win[None]]
    xs_new = _post_block(xs, attn, w_out, norm_g, (gt1, sc2, sh2, gt2), w_up, w_down, layer, mode="sum", tm=n)
    return xp_new, xs_new, state_p, (kv_cmp_s, kv_slc_s)


def kernel(x_prompt, x_sample, cache_swa_kv, cache_dil_kv, cache_nsa_cmp, cache_nsa_slc, cache_nsa_win,
           page_table, c_prompt, c_sample, rel_bias, ada_w, ada_b, norm_g, mlp_up, mlp_down,
           even_w_in, even_w_out, attn_sinks, nsa_w_in, nsa_w_out, cmp_w1, cmp_w2, cmp_pe):
    depth, d, _ = ada_w.shape
    b, l, _ = x_prompt.shape
    n = x_sample.shape[0]
    assert x_sample.shape[1] == 1
    assert cache_swa_kv.shape[2] == WIN_A and cache_dil_kv.shape[2] == DIL_MAX_WIN and cache_nsa_win.shape[2] == WIN_C

    ada_wb = ada_w.astype(BF16)
    up_b = mlp_up.astype(BF16)
    down_b = mlp_down.astype(BF16)
    even_in_b = even_w_in.astype(BF16)
    even_out_b = even_w_out.astype(BF16)
    nsa_in_b = jnp.pad(nsa_w_in, ((0, 0), (0, 0), (0, sum(NSA_SPLITS) - nsa_w_in.shape[2]))).astype(BF16)
    nsa_out_b = nsa_w_out.astype(BF16)
    q_w = NSA_HEADS * HEAD_DIM
    v_slc_cols = nsa_w_in[:, :, q_w + NSA_KV_W + NSA_KV_W // 2: q_w + 2 * NSA_KV_W]
    v_win_cols = nsa_w_in[:, :, q_w + 2 * NSA_KV_W + NSA_KV_W // 2: q_w + 3 * NSA_KV_W]
    gate_cols = nsa_w_in[:, :, q_w + 3 * NSA_KV_W:].reshape(-1, d, NSA_HEADS, 3).transpose(0, 1, 3, 2)
    nsa_in_t = jnp.concatenate([v_slc_cols, v_win_cols, gate_cols.reshape(-1, d, 3 * NSA_HEADS)], axis=2)
    nsa_in_t = nsa_in_t.transpose(0, 2, 1).astype(BF16)
    a_w = A_HEADS * HEAD_DIM
    even_in_t = even_w_in[:, :, a_w + A_KV_HEADS * HEAD_DIM: a_w + 2 * A_KV_HEADS * HEAD_DIM]
    even_in_t = even_in_t.transpose(0, 2, 1).astype(BF16)
    w1big, w2big, w1p = _compress_weights(cmp_w1, cmp_w2)
    pe_rows = _pe_rows(cmp_pe)

    c_all = jnp.concatenate([c_prompt, c_sample], axis=0)
    mods = [_linear(c_all, ada_wb, layer, ada_b[:, None, :], pre="silu", tm=b + n, tn=1536)
            for layer in range(depth)]

    nq = l // Q_BLOCK
    nch = l // CMP_STRIDE
    n_slc_pad = _round_up(-(-l // SLC_BLOCK), LANES)
    past = page_table.shape[1] * cache_nsa_cmp.shape[2]
    nch_s = past // CMP_STRIDE
    nb_past = past // SLC_BLOCK
    ov = _overlap_table(nch, n_slc_pad)
    pad = nch - Q_BLOCK // CMP_STRIDE
    page = cache_nsa_slc.shape[2]
    slc_s_dist = past - (jnp.arange(past // page)[:, None] * page + jnp.arange(page)[None, :])
    tabs = {
        "rel": rel_bias,
        "swa_t": _banded_bias_t(rel_bias[:, :A_HEADS], WIN_A),
        "dil": [_banded_bias(rel_bias[:, A_HEADS:], w // dd, dd) for w, dd in DIL_PAIRS],
        "win_t": _banded_bias_t(rel_bias, WIN_C),
        "cmp_rel_t": _cmp_rel_bias_t(rel_bias, nch),
        "ov_pad": jnp.pad(ov, ((pad, 2 * nch - pad - nch), (0, 0))),
        "slc_tiles": _slc_bias_tiles(rel_bias, nq),
        "cmp_s": _bias_of(rel_bias, past - (jnp.arange(nch_s) * CMP_STRIDE + CMP_LEN - 1)),
        "ov_s": _overlap_table(nch_s, _round_up(nb_past + 1, LANES)),
        "slc_s": _bias_of(rel_bias, slc_s_dist).transpose(1, 0, 2),
        "slc_s_new": _bias_of(rel_bias, jnp.zeros((1,), jnp.int32)),
    }
    views = {"swa": _cache_view(cache_swa_kv), "dil": _cache_view(cache_dil_kv), "win": _cache_view(cache_nsa_win),
             "cmp": _cache_view(cache_nsa_cmp), "slc": _cache_view(cache_nsa_slc)}
    bufs = {k: jnp.zeros(views[k].shape, F32) for k in ("swa", "dil", "win")}

    xp = x_prompt
    xs = x_sample.reshape(1, n, d)
    outs = {k: [] for k in ("swa_p", "dil_p", "cmp_p", "cmp_s", "slc_p", "slc_s", "win_p")}
    for layer in range(depth):
        li = layer // 2
        mod = mods[layer].reshape(b + n, 6, d)
        mod_p, mod_s = mod[:b], mod[b:][None]
        if layer % 2 == 0:
            wts = (even_in_b, even_in_t, even_out_b, attn_sinks, norm_g, up_b, down_b)
            xp, xs, (sa, sb) = _even_layer(xp, xs, mod_p, mod_s, views["swa"], views["dil"], bufs, li,
                                           layer, wts, tabs)
            outs["swa_p"].append(sa.reshape(b, -1, 2, A_KV_HEADS, HEAD_DIM))
            outs["dil_p"].append(sb.reshape(b, -1, 2, B_HEADS, HEAD_DIM))
        else:
            wts = (nsa_in_b, nsa_in_t, nsa_out_b, w1big, w2big, w1p, pe_rows, norm_g, up_b, down_b)
            xp, xs, (pc, ps, pw), (qc, qs_) = _nsa_layer(xp, xs, mod_p, mod_s, views["cmp"], views["slc"],
                                                         views["win"], bufs, page_table, li, layer, wts, tabs)
            kvshape = (2, NSA_KV_HEADS, HEAD_DIM)
            outs["cmp_p"].append(pc.reshape(b, -1, *kvshape))
            outs["slc_p"].append(ps.reshape(b, -1, *kvshape))
            outs["win_p"].append(pw.reshape(b, -1, *kvshape))
            outs["cmp_s"].append(qc.reshape(n, 1, *kvshape))
            outs["slc_s"].append(qs_.reshape(n, 1, *kvshape))
    st = {k: jnp.stack(v) for k, v in outs.items()}
    swa_s = _cache_unview(bufs["swa"], A_KV_HEADS)
    dil_s = _cache_unview(bufs["dil"], B_HEADS)
    win_s = _cache_unview(bufs["win"], NSA_KV_HEADS)
    return (xp, xs.reshape(n, 1, d), st["swa_p"], swa_s, st["dil_p"], dil_s,
            st["cmp_p"], st["cmp_s"], st["slc_p"], st["slc_s"], st["win_p"], win_s)
```

```python
import functools
import math

import jax
import jax.numpy as jnp
from jax import lax
from jax.experimental import pallas as pl
from jax.experimental.pallas import tpu as pltpu

F32 = jnp.float32
BF16 = jnp.bfloat16

HEAD_DIM = 64
A_HEADS = 8
A_KV_HEADS = 2
B_HEADS = 8
WIN_A = 128
DIL_PAIRS = ((128, 1), (512, 4), (2048, 16))
DIL_MAX_WIN = 2048
NSA_HEADS = 16
NSA_KV_HEADS = 2
CMP_STRIDE = 16
CMP_LEN = 32
CMP_HID = 128
SLC_BLOCK = 64
N_SELECT = 16
WIN_C = 512
RP_BUCKETS = 32
RP_MAX_DIST = 2048
Q_BLOCK = 128
EPS = 1e-6
NEG = -1e30
REMOVED = -3e38
FORCE = 1e6
SCALE = HEAD_DIM ** -0.5
CMP_SAMPLE_GROUP = 8
SLC_TILES_PER_STEP = 4

VMEM_LIMIT_BYTES = 56 * 1024 * 1024
LANES = 128


def _cparams(*sem):
    return pltpu.CompilerParams(dimension_semantics=sem, vmem_limit_bytes=VMEM_LIMIT_BYTES)


def _round_up(x, m):
    return (x + m - 1) // m * m


def _t5_bucket(dist):
    n = dist.astype(jnp.int32)
    exact = RP_BUCKETS // 2
    val = jnp.log(jnp.maximum(n, 1).astype(F32) / exact) / math.log(RP_MAX_DIST / exact)
    large = jnp.minimum(exact + (val * (RP_BUCKETS - exact)).astype(jnp.int32), RP_BUCKETS - 1)
    return jnp.where(n < exact, n, large)


def _bias_of(tab, dist):
    b = tab[_t5_bucket(jnp.maximum(dist, 0))].astype(F32)
    return jnp.moveaxis(b, -1, 0)


def _toeplitz(v, rows, cols):
    lv = v.shape[-1]
    assert lv > cols and lv >= rows + cols - 1
    flat = jnp.tile(v, (1,) * (v.ndim - 1) + (rows,))[..., :rows * (lv - 1)]
    return flat.reshape(v.shape[:-1] + (rows, lv - 1))[..., :cols]


def _signed_index(lv, n_pos):
    u = jnp.arange(lv)
    return jnp.where(u < n_pos, u, u - lv)


def _rms(x, g):
    return x * lax.rsqrt(jnp.mean(x * x, axis=-1, keepdims=True) + EPS) * g


def _dot(a, b):
    return jnp.dot(a, b, preferred_element_type=F32)


def _dot_nt(a, b):
    return lax.dot_general(a, b, (((1,), (1,)), ((), ())), preferred_element_type=F32)


def _split_bf16(x):
    hi = x.astype(BF16)
    lo = (x - hi.astype(F32)).astype(BF16)
    return hi, lo


def _dot_nt_3pass(a, b):
    ah, al = _split_bf16(a)
    bh, bl = _split_bf16(b)
    return _dot_nt(ah, bh) + (_dot_nt(ah, bl) + _dot_nt(al, bh))


def _dot_2pass_lhs(a, b_bf16):
    ah, al = _split_bf16(a)
    return _dot(ah, b_bf16) + _dot(al, b_bf16)


def _gelu_tanh(x):
    return 0.5 * x * (1.0 + jnp.tanh(math.sqrt(2.0 / math.pi) * (x + 0.044715 * (x * x * x))))


def _linear_kernel(*refs, pre, has_bias):
    if has_bias:
        a_ref, b_ref, bias_ref, o_ref = refs
    else:
        a_ref, b_ref, o_ref = refs
    a = a_ref[...]
    if pre == "silu":
        a = a * jax.nn.sigmoid(a)
    acc = _dot(a.astype(BF16), b_ref[...])
    if has_bias:
        acc = acc + bias_ref[...]
    o_ref[...] = acc


def _linear(a, b, layer, bias=None, *, pre=None, tm, tn):
    m, k = a.shape
    n = b.shape[2]
    assert m % tm == 0 and n % tn == 0
    in_specs = [pl.BlockSpec((tm, k), lambda i, j: (i, 0)),
                pl.BlockSpec((None, k, tn), lambda i, j: (layer, 0, j))]
    args = [a, b]
    if bias is not None:
        in_specs.append(pl.BlockSpec((None, 1, tn), lambda i, j: (layer, 0, j)))
        args.append(bias)
    return pl.pallas_call(
        functools.partial(_linear_kernel, pre=pre, has_bias=bias is not None),
        grid=(m // tm, n // tn),
        in_specs=in_specs,
        out_specs=pl.BlockSpec((tm, tn), lambda i, j: (i, j)),
        out_shape=jax.ShapeDtypeStruct((m, n), F32),
        compiler_params=_cparams("parallel", "parallel"),
        name="linear",
    )(*args)


def _norm_linear_kernel(*refs, splits, splits_t):
    x_ref, g_ref, sc_ref, sh_ref, w_ref = refs[:5]
    rest = refs[5:]
    if splits_t:
        wt_ref, rest = rest[0], rest[1:]
    o_refs = rest[:len(splits)]
    ot_refs = rest[len(splits):]
    h = _rms(x_ref[...], g_ref[...]) * (1.0 + sc_ref[...]) + sh_ref[...]
    hb = h.astype(BF16)
    off = 0
    for o_ref, sz in zip(o_refs, splits):
        o_ref[...] = _dot(hb, w_ref[:, off:off + sz])
        off += sz
    off = 0
    for o_ref, sz in zip(ot_refs, splits_t):
        o_ref[...] = _dot_nt(wt_ref[off:off + sz, :], hb)
        off += sz


def _norm_linear(x, g, sc, sh, w, layer, splits, *, tm, w_t=None, splits_t=()):
    b, l, d = x.shape
    n = w.shape[2]
    assert sum(splits) == n and l % tm == 0
    r = sc.shape[1]
    rb = 1 if r == 1 else tm
    mod_map = (lambda bi, i: (bi, 0, 0)) if r == 1 else (lambda bi, i: (bi, i, 0))
    in_specs = [pl.BlockSpec((None, tm, d), lambda bi, i: (bi, i, 0)),
                pl.BlockSpec((1, d), lambda bi, i: (0, 0)),
                pl.BlockSpec((None, rb, d), mod_map),
                pl.BlockSpec((None, rb, d), mod_map),
                pl.BlockSpec((None, d, n), lambda bi, i: (layer, 0, 0))]
    args = [x, g, sc, sh, w]
    if splits_t:
        assert sum(splits_t) == w_t.shape[1]
        in_specs.append(pl.BlockSpec((None, w_t.shape[1], d), lambda bi, i: (layer, 0, 0)))
        args.append(w_t)
    return pl.pallas_call(
        functools.partial(_norm_linear_kernel, splits=splits, splits_t=splits_t),
        grid=(b, l // tm),
        in_specs=in_specs,
        out_specs=[pl.BlockSpec((None, tm, sz), lambda bi, i: (bi, i, 0)) for sz in splits]
        + [pl.BlockSpec((None, sz, tm), lambda bi, i: (bi, 0, i)) for sz in splits_t],
        out_shape=[jax.ShapeDtypeStruct((b, l, sz), F32) for sz in splits]
        + [jax.ShapeDtypeStruct((b, sz, l), F32) for sz in splits_t],
        compiler_params=_cparams("parallel", "parallel"),
        name="norm_linear",
    )(*args)


def _banded_bias(tab, window, dist_scale):
    nb = -(-window // Q_BLOCK)
    kw = (nb + 1) * Q_BLOCK
    lv = kw + Q_BLOCK
    dist = nb * Q_BLOCK - _signed_index(lv, kw)
    v = _bias_of(tab, jnp.maximum(dist, 0) * dist_scale)
    return _toeplitz(v, Q_BLOCK, kw)


def _banded_gqa_kernel(*refs, hq, nb, window, has_sink, gate_row):
    it = iter(refs)
    q_ref = next(it)
    k_refs = [next(it) for _ in range(nb + 1)]
    vt_refs = [next(it) for _ in range(nb + 1)]
    bias_ref = next(it)
    sink_ref = next(it) if has_sink else None
    gate_ref = next(it) if gate_row is not None else None
    o_ref, qt_ref = next(it), next(it)
    hkv = 2
    g = hq // hkv
    pairs = g // 2
    kw = (nb + 1) * Q_BLOCK
    i = pl.program_id(1)
    k2 = jnp.concatenate([kr[...] for kr in k_refs], axis=0).astype(BF16)
    vt2 = jnp.concatenate([vr[...] for vr in vt_refs], axis=1).astype(BF16)
    c = lax.broadcasted_iota(jnp.int32, (kw, Q_BLOCK), 0)
    r = lax.broadcasted_iota(jnp.int32, (kw, Q_BLOCK), 1)
    dist = r + nb * Q_BLOCK - c
    ok = (dist >= 0) & (dist <= window) & ((i - nb) * Q_BLOCK + c >= 0)
    drop = jnp.where(ok, 0.0, NEG)
    for kh in range(hkv):
        q = q_ref[:, kh * g * HEAD_DIM:(kh + 1) * g * HEAD_DIM] * SCALE
        qt_ref[...] = jnp.zeros(qt_ref.shape, BF16)
        for pr in range(pairs):
            t = q[:, pr * LANES:(pr + 1) * LANES].T.astype(BF16)
            for half in range(2):
                h = 2 * pr + half
                qt_ref[kh * HEAD_DIM:(kh + 1) * HEAD_DIM, h * Q_BLOCK:(h + 1) * Q_BLOCK] = (
                    t[half * HEAD_DIM:(half + 1) * HEAD_DIM, :])
        st = _dot(k2, qt_ref[...]) + bias_ref[kh]
        vt = vt2[kh * HEAD_DIM:(kh + 1) * HEAD_DIM, :]
        if gate_ref is not None:
            gates = jax.nn.sigmoid(gate_ref[gate_row + kh * g:gate_row + (kh + 1) * g, :])
        outs = []
        for h in range(g):
            s_h = st[:, h * Q_BLOCK:(h + 1) * Q_BLOCK] + drop
            m = jnp.max(s_h, axis=0, keepdims=True)
            p = jnp.exp(s_h - m)
            l = jnp.sum(p, axis=0, keepdims=True)
            acc = _dot(vt, p.astype(BF16))
            if has_sink:
                sk = sink_ref[kh * g + h]
                big = jnp.maximum(m, sk)
                w = jnp.exp(m - big)
                o = (acc * w) / (l * w + jnp.exp(sk - big))
            else:
                o = acc / l
            if gate_ref is not None:
                o = o * gates[h:h + 1, :]
            outs.append(o)
        for pr in range(pairs):
            col = (kh * pairs + pr) * LANES
            o_ref[:, col:col + LANES] = jnp.concatenate(outs[2 * pr:2 * pr + 2], axis=0).T


def _banded_gqa(q, kv, vt, bias_t, *, hq, window, sink=None, gate_t=None, gate_row=None):
    b, l, wq = q.shape
    wk = 2 * HEAD_DIM
    g = hq // 2
    nq = l // Q_BLOCK
    nb = -(-window // Q_BLOCK)
    in_specs = [pl.BlockSpec((None, Q_BLOCK, wq), lambda bi, i: (bi, i, 0))]
    args = [q]
    for j in range(nb + 1):
        in_specs.append(pl.BlockSpec((None, Q_BLOCK, wk), lambda bi, i, j=j: (bi, jnp.maximum(i - nb + j, 0), 0)))
        args.append(kv)
    for j in range(nb + 1):
        in_specs.append(pl.BlockSpec((None, wk, Q_BLOCK), lambda bi, i, j=j: (bi, 0, jnp.maximum(i - nb + j, 0))))
        args.append(vt)
    in_specs.append(pl.BlockSpec(bias_t.shape, lambda bi, i: (0, 0, 0)))
    args.append(bias_t)
    if sink is not None:
        in_specs.append(pl.BlockSpec(memory_space=pltpu.SMEM))
        args.append(sink)
    if gate_t is not None:
        in_specs.append(pl.BlockSpec((None, gate_t.shape[1], Q_BLOCK), lambda bi, i: (bi, 0, i)))
        args.append(gate_t)
    return pl.pallas_call(
        functools.partial(_banded_gqa_kernel, hq=hq, nb=nb, window=window, has_sink=sink is not None,
                          gate_row=gate_row if gate_t is not None else None),
        grid=(b, nq),
        in_specs=in_specs,
        out_specs=pl.BlockSpec((None, Q_BLOCK, wq), lambda bi, i: (bi, i, 0)),
        out_shape=jax.ShapeDtypeStruct((b, l, wq), F32),
        scratch_shapes=[pltpu.VMEM((wk, g * Q_BLOCK), BF16)],
        compiler_params=_cparams("parallel", "parallel"),
        name="banded_gqa",
    )(*args)


def _dilated_kernel(q_ref, kp_ref, kc_ref, vp_ref, vc_ref, bias_ref, o_ref, acc_ref, m_ref, l_ref, *, patterns):
    s = pl.program_id(2)
    sb = q_ref.shape[0]
    lane_hi = lax.broadcasted_iota(jnp.int32, (Q_BLOCK, LANES), 1) >= HEAD_DIM
    rq = lax.broadcasted_iota(jnp.int32, (Q_BLOCK, 2 * Q_BLOCK), 0)
    ck = lax.broadcasted_iota(jnp.int32, (Q_BLOCK, 2 * Q_BLOCK), 1)
    dist = rq + Q_BLOCK - ck
    band = (dist >= 0) & (dist <= Q_BLOCK)
    band_first = band & ((ck >= Q_BLOCK) | (s > 0))
    for pi, (window, dil) in enumerate(patterns):
        assert window // dil == Q_BLOCK and sb % (dil * Q_BLOCK) == 0
        nsub = sb // (dil * Q_BLOCK)
        for r in range(dil):
            for t in range(nsub):
                cur = pl.ds(r + dil * Q_BLOCK * t, Q_BLOCK, stride=dil)
                if t == 0:
                    prev_k = kp_ref[pl.ds(r + sb - dil * Q_BLOCK, Q_BLOCK, stride=dil), :]
                    prev_v = vp_ref[pl.ds(r + sb - dil * Q_BLOCK, Q_BLOCK, stride=dil), :]
                else:
                    before = pl.ds(r + dil * Q_BLOCK * (t - 1), Q_BLOCK, stride=dil)
                    prev_k, prev_v = kc_ref[before, :], vc_ref[before, :]
                kcat = jnp.concatenate([prev_k, kc_ref[cur, :]], axis=0).astype(BF16)
                vcat = jnp.concatenate([prev_v, vc_ref[cur, :]], axis=0).astype(BF16)
                q = q_ref[cur, :] * SCALE
                mask = band_first if t == 0 else band
                parts = []
                for hh in range(2):
                    own = lane_hi if hh else jnp.logical_not(lane_hi)
                    qm = jnp.where(own, q, 0.0).astype(BF16)
                    sc = jnp.where(mask, _dot_nt(qm, kcat) + bias_ref[pi, hh], NEG)
                    m = jnp.max(sc, axis=-1, keepdims=True)
                    p = jnp.exp(sc - m)
                    parts.append((_dot(p.astype(BF16), vcat), m, jnp.sum(p, axis=-1, keepdims=True)))
                (a0, m0, l0), (a1, m1, l1) = parts
                acc_ref[pi, cur, :] = jnp.where(lane_hi, a1, a0)
                m_ref[pi, cur, :] = jnp.where(lane_hi, m1, m0)
                l_ref[pi, cur, :] = jnp.where(lane_hi, l1, l0)
    rows = 256
    for c0 in range(0, sb, rows):
        rs = slice(c0, c0 + rows)
        ms = [m_ref[pi, rs, :] for pi in range(len(patterns))]
        big = functools.reduce(jnp.maximum, ms)
        l_tot = 0.0
        acc_tot = 0.0
        for pi, m in enumerate(ms):
            w = jnp.exp(m - big)
            l_tot = l_tot + l_ref[pi, rs, :] * w
            acc_tot = acc_tot + acc_ref[pi, rs, :] * w
        o_ref[rs, :] = acc_tot / l_tot


def _dilated_attn(q, kv, bias, patterns):
    b, l, wq = q.shape
    n_pairs = wq // LANES
    sb = min(max(w for w, _ in patterns), l)
    assert l % sb == 0
    prev = lambda half: (lambda bi, p, s: (bi, jnp.maximum(s - 1, 0), half * n_pairs + p))
    cur = lambda half: (lambda bi, p, s: (bi, s, half * n_pairs + p))
    blk = (None, sb, LANES)
    return pl.pallas_call(
        functools.partial(_dilated_kernel, patterns=patterns),
        grid=(b, n_pairs, l // sb),
        in_specs=[pl.BlockSpec(blk, cur(0)),
                  pl.BlockSpec(blk, prev(0)), pl.BlockSpec(blk, cur(0)),
                  pl.BlockSpec(blk, prev(1)), pl.BlockSpec(blk, cur(1)),
                  pl.BlockSpec((len(patterns), 2, Q_BLOCK, 2 * Q_BLOCK), lambda bi, p, s: (0, p, 0, 0))],
        out_specs=pl.BlockSpec(blk, cur(0)),
        out_shape=jax.ShapeDtypeStruct((b, l, wq), F32),
        scratch_shapes=[pltpu.VMEM((len(patterns), sb, LANES), F32)] * 3,
        compiler_params=_cparams("parallel", "parallel", "parallel"),
        name="dilated_attn",
    )(q, kv, kv, kv, kv, bias)


def _banded_bias_t(tab, window):
    bias = _banded_bias(tab, window, 1)
    heads, _, kw = bias.shape
    g = heads // 2
    return bias.reshape(2, g, Q_BLOCK, kw).transpose(0, 3, 1, 2).reshape(2, kw, g * Q_BLOCK)


def _compress_finish_kernel(hab_ref, pe_ref, w2_ref, o_ref, *, nch, pad):
    hid_w = hab_ref.shape[1] // 2
    hab = hab_ref[...]
    ha = hab[:, :hid_w]
    hb_next = pltpu.roll(hab[:, hid_w:], nch - 1, axis=0)
    pe = pe_ref[0:1, :hid_w] + pe_ref[1:2, hid_w:]
    hid = _gelu_tanh(ha + hb_next + pe)
    ckv = _dot(hid.astype(BF16), w2_ref[...])
    if pad:
        o_ref[0:pad, :] = jnp.zeros((pad, o_ref.shape[1]), F32)
        o_ref[pad + nch:, :] = jnp.zeros((o_ref.shape[0] - pad - nch, o_ref.shape[1]), F32)
    o_ref[pad:pad + nch, :] = ckv


def _compress_finish(hab, pe_rows, w2big, layer, *, pad, rows_out):
    b, nch, w = hab.shape
    n_out = w2big.shape[2]
    return pl.pallas_call(
        functools.partial(_compress_finish_kernel, nch=nch, pad=pad),
        grid=(b,),
        in_specs=[pl.BlockSpec((None, nch, w), lambda bi: (bi, 0, 0)),
                  pl.BlockSpec(pe_rows.shape, lambda bi: (0, 0)),
                  pl.BlockSpec((None,) + w2big.shape[1:], lambda bi: (layer, 0, 0))],
        out_specs=pl.BlockSpec((None, rows_out, n_out), lambda bi: (bi, 0, 0)),
        out_shape=jax.ShapeDtypeStruct((b, rows_out, n_out), F32),
        compiler_params=_cparams("parallel"),
        name="compress_finish",
    )(hab, pe_rows, w2big)


def _compress_weights(cmp_w1, cmp_w2):
    lyr = cmp_w1.shape[0]
    eye_h = jnp.eye(NSA_KV_HEADS, dtype=F32)
    eye_p = jnp.eye(2, dtype=F32)
    w1 = cmp_w1.reshape(lyr, 2, 2, CMP_STRIDE, HEAD_DIM, CMP_HID)
    w1big = jnp.einsum("lpasde,pq,hk->lsphdaqke", w1, eye_p, eye_h)
    w1big = w1big.reshape(lyr, CMP_STRIDE * 2 * NSA_KV_HEADS * HEAD_DIM, 2 * 2 * NSA_KV_HEADS * CMP_HID)
    w2big = jnp.einsum("lped,pq,hk->lpheqkd", cmp_w2, eye_p, eye_h)
    w2big = w2big.reshape(lyr, 2 * NSA_KV_HEADS * CMP_HID, 2 * NSA_KV_HEADS * HEAD_DIM)
    w1p = jnp.einsum("lpasde,hk->lsphdake", w1, eye_h)
    w1p = w1p.reshape(lyr, CMP_STRIDE // 2, 2, 2, NSA_KV_HEADS * HEAD_DIM, 2 * NSA_KV_HEADS * CMP_HID)
    w1p = w1p.transpose(0, 1, 3, 2, 4, 5).reshape(lyr, CMP_STRIDE // 2, 2, 2 * NSA_KV_HEADS * HEAD_DIM,
                                                  2 * NSA_KV_HEADS * CMP_HID)
    return w1big.astype(BF16), w2big.astype(BF16), w1p.astype(BF16)


def _pe_rows(cmp_pe):
    lyr = cmp_pe.shape[0]
    pe = cmp_pe.reshape(lyr, 2, 2, CMP_STRIDE, HEAD_DIM)
    pe = jnp.broadcast_to(pe[:, :, :, :, None, :], (lyr, 2, 2, CMP_STRIDE, NSA_KV_HEADS, HEAD_DIM))
    pe = pe.transpose(0, 2, 3, 1, 4, 5).reshape(lyr, 2, CMP_STRIDE * 2 * NSA_KV_HEADS * HEAD_DIM)
    return jnp.pad(pe, ((0, 0), (0, 6), (0, 0)))


def _overlap_table(nch, n_slc_pad):
    c = jnp.arange(nch)[:, None]
    jb = jnp.arange(n_slc_pad)[None, :]
    cstart = c * CMP_STRIDE
    cend = cstart + CMP_LEN - 1
    ov = (cstart <= jb * SLC_BLOCK + SLC_BLOCK - 1) & (cend >= jb * SLC_BLOCK) & (c < nch - 1)
    return ov.astype(BF16)


def _topk_select(imp, k_sel):
    lane = lax.broadcasted_iota(jnp.int32, imp.shape, 1).astype(F32)
    sel = jnp.zeros(imp.shape, F32)
    work = imp
    picks = []
    for _ in range(k_sel):
        mx = jnp.max(work, axis=-1, keepdims=True)
        idx = jnp.min(jnp.where(work == mx, lane, float(imp.shape[1])), axis=-1, keepdims=True)
        pick = lane == idx
        sel = jnp.where(pick, 1.0, sel)
        work = jnp.where(pick, REMOVED, work)
        picks.append(idx)
    return sel, picks


def _cmp_prompt_kernel(q_ref, ckv_ref, bias_ref, ov_ref, gate_ref, o_ref, imp_ref, qh_ref, ql_ref, *, nch, n_slc):
    i = pl.program_id(1)
    g = NSA_HEADS // NSA_KV_HEADS
    wk = NSA_KV_HEADS * HEAD_DIM
    pairs = g // 2
    cpq = Q_BLOCK // CMP_STRIDE
    start = pl.multiple_of(i * cpq, 8)
    win = ckv_ref[pl.ds(start, nch), :]
    ov = ov_ref[pl.ds(start, nch), :]
    kh_hi, kh_lo = _split_bf16(win[:, :wk])
    vt2 = win[:, wk:].T.astype(BF16)
    w = lax.broadcasted_iota(jnp.int32, (nch, Q_BLOCK), 0)
    r = lax.broadcasted_iota(jnp.int32, (nch, Q_BLOCK), 1)
    e = w - (nch - cpq)
    ok = (r - CMP_STRIDE * e - (CMP_LEN - 1) >= 0) & (e + i * cpq >= 0)
    keep = jnp.where(ok, 1.0, 0.0)
    drop = (keep - 1.0) * (-NEG)

    n_slc_pad = ov_ref.shape[1]
    jb = lax.broadcasted_iota(jnp.int32, (Q_BLOCK, n_slc_pad), 1)
    qpos = i * Q_BLOCK + lax.broadcasted_iota(jnp.int32, (Q_BLOCK, n_slc_pad), 0)
    cur = qpos // SLC_BLOCK
    valid = (jb * SLC_BLOCK <= qpos) & (jb < n_slc)
    forced = (jb == 0) | (jb == cur) | (jb == cur - 1)

    for kh in range(NSA_KV_HEADS):
        q = q_ref[:, kh * g * HEAD_DIM:(kh + 1) * g * HEAD_DIM] * SCALE
        qh_ref[...] = jnp.zeros(qh_ref.shape, BF16)
        ql_ref[...] = jnp.zeros(ql_ref.shape, BF16)
        for pr in range(pairs):
            t_hi, t_lo = _split_bf16(q[:, pr * LANES:(pr + 1) * LANES].T)
            for half in range(2):
                h = 2 * pr + half
                rows = slice(kh * HEAD_DIM, (kh + 1) * HEAD_DIM)
                src = slice(half * HEAD_DIM, (half + 1) * HEAD_DIM)
                qh_ref[rows, h * Q_BLOCK:(h + 1) * Q_BLOCK] = t_hi[src, :]
                ql_ref[rows, h * Q_BLOCK:(h + 1) * Q_BLOCK] = t_lo[src, :]
        st = _dot(kh_hi, qh_ref[...]) + (_dot(kh_hi, ql_ref[...]) + _dot(kh_lo, qh_ref[...]))
        st = st + bias_ref[kh]
        vt = vt2[kh * HEAD_DIM:(kh + 1) * HEAD_DIM, :]
        gates = jax.nn.sigmoid(gate_ref[kh * g:(kh + 1) * g, :])
        psum_t = jnp.zeros((nch, Q_BLOCK), F32)
        outs = []
        for h in range(g):
            s_h = st[:, h * Q_BLOCK:(h + 1) * Q_BLOCK] + drop
            m = jnp.max(s_h, axis=0, keepdims=True)
            p = jnp.exp(s_h - m) * keep
            p = p / jnp.maximum(jnp.sum(p, axis=0, keepdims=True), 1e-30)
            outs.append(_dot(vt, p.astype(BF16)) * gates[h:h + 1, :])
            psum_t = psum_t + p
        for pr in range(pairs):
            col = (kh * pairs + pr) * LANES
            o_ref[:, col:col + LANES] = jnp.concatenate(outs[2 * pr:2 * pr + 2], axis=0).T
        imp = _dot_2pass_lhs(psum_t.T, ov)
        imp_ref[kh] = jnp.where(valid, imp + jnp.where(forced, FORCE, 0.0), NEG)


def _cmp_prompt(q, ckv_pad, bias_rel_t, ov_pad, gate_t, *, nch, n_slc):
    b, l, wq = q.shape
    nq = l // Q_BLOCK
    n_slc_pad = ov_pad.shape[1]
    ql = (NSA_HEADS // NSA_KV_HEADS) * Q_BLOCK
    wk = NSA_KV_HEADS * HEAD_DIM
    return pl.pallas_call(
        functools.partial(_cmp_prompt_kernel, nch=nch, n_slc=n_slc),
        grid=(b, nq),
        in_specs=[pl.BlockSpec((None, Q_BLOCK, wq), lambda bi, i: (bi, i, 0)),
                  pl.BlockSpec((None,) + ckv_pad.shape[1:], lambda bi, i: (bi, 0, 0)),
                  pl.BlockSpec(bias_rel_t.shape, lambda bi, i: (0, 0, 0)),
                  pl.BlockSpec(ov_pad.shape, lambda bi, i: (0, 0)),
                  pl.BlockSpec((None, NSA_HEADS, Q_BLOCK), lambda bi, i: (bi, 0, i))],
        out_specs=[pl.BlockSpec((None, Q_BLOCK, wq), lambda bi, i: (bi, i, 0)),
                   pl.BlockSpec((None, NSA_KV_HEADS, Q_BLOCK, n_slc_pad), lambda bi, i: (bi, 0, i, 0))],
        out_shape=[jax.ShapeDtypeStruct((b, l, wq), F32),
                   jax.ShapeDtypeStruct((b, NSA_KV_HEADS, l, n_slc_pad), F32)],
        scratch_shapes=[pltpu.VMEM((wk, ql), BF16), pltpu.VMEM((wk, ql), BF16)],
        compiler_params=_cparams("parallel", "parallel"),
        name="cmp_prompt",
    )(q, ckv_pad, bias_rel_t, ov_pad, gate_t)


def _cmp_rel_bias_t(tab, nch):
    heads = tab.shape[1]
    g = NSA_HEADS // NSA_KV_HEADS
    cpq = Q_BLOCK // CMP_STRIDE
    lv = nch + cpq
    u = _signed_index(lv, nch)
    dist = CMP_STRIDE * ((nch - cpq) - u[None, :]) + jnp.arange(CMP_STRIDE)[:, None] - (CMP_LEN - 1)
    t = _toeplitz(_bias_of(tab, jnp.maximum(dist, 0)), cpq, nch)
    t = t.transpose(0, 2, 1, 3).reshape(NSA_KV_HEADS, g, Q_BLOCK, nch)
    return t.transpose(0, 3, 1, 2).reshape(NSA_KV_HEADS, nch, g * Q_BLOCK)


def _topk_rows_kernel(imp_ref, sel_ref, *, k_sel):
    sel, _ = _topk_select(imp_ref[...], k_sel)
    sel_ref[...] = sel.astype(BF16)


def _topk_rows(imp, k_sel, *, tr):
    rows, c = imp.shape
    assert rows % tr == 0
    return pl.pallas_call(
        functools.partial(_topk_rows_kernel, k_sel=k_sel),
        grid=(rows // tr,),
        in_specs=[pl.BlockSpec((tr, c), lambda i: (i, 0))],
        out_specs=pl.BlockSpec((tr, c), lambda i: (i, 0)),
        out_shape=jax.ShapeDtypeStruct((rows, c), BF16),
        compiler_params=_cparams("parallel"),
        name="topk_rows",
    )(imp)


def _slc_prompt_kernel(q_ref, k_ref, vt_ref, sel_ref, bias_ref, gate_ref, o_ref,
                       qt_ref, m_ref, l_ref, acc_ref, *, nd):
    kvh = pl.program_id(1)
    i = pl.program_id(2)
    g = NSA_HEADS // NSA_KV_HEADS
    ql = g * Q_BLOCK
    pairs = g // 2
    q = q_ref[...] * SCALE
    qt_ref[...] = jnp.zeros(qt_ref.shape, BF16)
    row0 = pl.multiple_of(kvh * HEAD_DIM, HEAD_DIM)
    for pr in range(pairs):
        t = q[:, pr * LANES:(pr + 1) * LANES].T
        for half in range(2):
            h = 2 * pr + half
            qt_ref[pl.ds(row0, HEAD_DIM), h * Q_BLOCK:(h + 1) * Q_BLOCK] = (
                t[half * HEAD_DIM:(half + 1) * HEAD_DIM, :].astype(BF16))
    m_ref[...] = jnp.full(m_ref.shape, NEG, F32)
    l_ref[...] = jnp.zeros(l_ref.shape, F32)
    acc_ref[...] = jnp.zeros(acc_ref.shape, F32)

    sel = sel_ref[...]
    n_slc_pad = sel.shape[1]
    key_blk = lax.broadcasted_iota(jnp.int32, (Q_BLOCK, n_slc_pad), 0) // SLC_BLOCK
    blk_id = lax.broadcasted_iota(jnp.int32, (Q_BLOCK, n_slc_pad), 1)
    kk = lax.broadcasted_iota(jnp.int32, (Q_BLOCK, Q_BLOCK), 0)
    qq = lax.broadcasted_iota(jnp.int32, (Q_BLOCK, Q_BLOCK), 1)
    bpt = Q_BLOCK // SLC_BLOCK

    def body(jj, carry):
        k_parts, v_parts, b_parts, d_parts = [], [], [], []
        for sub in range(SLC_TILES_PER_STEP):
            j = jj * SLC_TILES_PER_STEP + sub
            jc = jnp.minimum(j, i)
            col0 = pl.multiple_of(jc * Q_BLOCK, Q_BLOCK)
            v_parts.append(vt_ref[pl.ds(row0, HEAD_DIM), pl.ds(col0, Q_BLOCK)].astype(BF16))
            b_parts.append(bias_ref[jnp.minimum(i - jc, nd)])
            expand = jnp.where(blk_id == jc * bpt + key_blk, 1.0, 0.0).astype(BF16)
            picked = _dot_nt(expand, sel) > 0.5
            d_parts.append(jnp.where(picked & ((kk <= qq) | (j < i)) & (j <= i), 0.0, NEG))
            k_parts.append(k_ref[pl.ds(col0, Q_BLOCK), :].astype(BF16))
        drop = jnp.concatenate(d_parts, axis=0)
        st = _dot(jnp.concatenate(k_parts, axis=0), qt_ref[...]) + jnp.concatenate(b_parts, axis=0)
        vt = jnp.concatenate(v_parts, axis=1)
        for h in range(g):
            hs = slice(h * Q_BLOCK, (h + 1) * Q_BLOCK)
            s_h = st[:, hs] + drop
            m_prev = m_ref[:, hs]
            m_new = jnp.maximum(m_prev, jnp.max(s_h, axis=0, keepdims=True))
            alpha = jnp.exp(m_prev - m_new)
            p = jnp.exp(s_h - m_new)
            l_ref[:, hs] = alpha * l_ref[:, hs] + jnp.sum(p, axis=0, keepdims=True)
            acc_ref[:, hs] = alpha * acc_ref[:, hs] + _dot(vt, p.astype(BF16))
            m_ref[:, hs] = m_new
        return carry

    lax.fori_loop(0, (i + SLC_TILES_PER_STEP) // SLC_TILES_PER_STEP, body, 0)
    ot = acc_ref[...] / l_ref[...]
    grow0 = pl.multiple_of(NSA_HEADS + kvh * g, g)
    gates = jax.nn.sigmoid(gate_ref[pl.ds(grow0, g), :])
    for pr in range(pairs):
        parts = []
        for half in range(2):
            h = 2 * pr + half
            parts.append(ot[:, h * Q_BLOCK:(h + 1) * Q_BLOCK] * gates[h:h + 1, :])
        o_ref[:, pr * LANES:(pr + 1) * LANES] = jnp.concatenate(parts, axis=0).T


def _slc_prompt(q, kv_slc, vt_slc, sel, bias_tiles, gate_t):
    b, l, wq = q.shape
    nq = l // Q_BLOCK
    g = NSA_HEADS // NSA_KV_HEADS
    wg = g * HEAD_DIM
    wk = NSA_KV_HEADS * HEAD_DIM
    nd = bias_tiles.shape[0] - 1
    n_slc_pad = sel.shape[3]
    ql = g * Q_BLOCK
    return pl.pallas_call(
        functools.partial(_slc_prompt_kernel, nd=nd),
        grid=(b, NSA_KV_HEADS, nq),
        in_specs=[pl.BlockSpec((None, Q_BLOCK, wg), lambda bi, kh, i: (bi, i, kh)),
                  pl.BlockSpec((None, l, wk), lambda bi, kh, i: (bi, 0, 0)),
                  pl.BlockSpec((None, wk, l), lambda bi, kh, i: (bi, 0, 0)),
                  pl.BlockSpec((None, None, Q_BLOCK, n_slc_pad), lambda bi, kh, i: (bi, kh, i, 0)),
                  pl.BlockSpec((nd + 1, None, Q_BLOCK, ql), lambda bi, kh, i: (0, kh, 0, 0)),
                  pl.BlockSpec((None, gate_t.shape[1], Q_BLOCK), lambda bi, kh, i: (bi, 0, i))],
        out_specs=pl.BlockSpec((None, Q_BLOCK, wg), lambda bi, kh, i: (bi, i, kh)),
        out_shape=jax.ShapeDtypeStruct((b, l, wq), F32),
        scratch_shapes=[pltpu.VMEM((wk, ql), BF16), pltpu.VMEM((1, ql), F32), pltpu.VMEM((1, ql), F32),
                        pltpu.VMEM((HEAD_DIM, ql), F32)],
        compiler_params=_cparams("parallel", "parallel", "arbitrary"),
        name="slc_prompt",
    )(q, kv_slc, vt_slc, sel, bias_tiles, gate_t)


def _slc_bias_tiles(tab, nq):
    nd = min(nq, 13)
    heads = tab.shape[1]
    g = NSA_HEADS // NSA_KV_HEADS
    t = _bias_of(tab, jnp.arange((nd + 1) * Q_BLOCK))
    rel = _signed_index(2 * Q_BLOCK, Q_BLOCK)
    idx = jnp.maximum(jnp.arange(nd)[:, None] * Q_BLOCK + rel[None, :], 0)
    near = _toeplitz(t[:, idx], Q_BLOCK, Q_BLOCK)
    far = jnp.broadcast_to(tab[RP_BUCKETS - 1].astype(F32)[:, None, None, None], (heads, 1, Q_BLOCK, Q_BLOCK))
    tiles = jnp.concatenate([near, far], axis=1)
    tiles = tiles.reshape(NSA_KV_HEADS, g, nd + 1, Q_BLOCK, Q_BLOCK).transpose(2, 0, 3, 1, 4)
    return tiles.reshape(nd + 1, NSA_KV_HEADS, Q_BLOCK, g * Q_BLOCK)


def _post_kernel(*refs, mode, n_attn, ff_chunk):
    it = iter(refs)
    x_ref = next(it)
    branch_refs = [next(it) for _ in range(n_attn)]
    wout_ref, g_ref, gt1_ref, sc2_ref, sh2_ref, gt2_ref, wup_ref, wdown_ref, o_ref = it

    if mode == "cat":
        o = jnp.concatenate([br[...] for br in branch_refs], axis=1)
    else:
        assert mode == "sum"
        o = branch_refs[0][...]
        for br in branch_refs[1:]:
            o = o + br[...]

    y = _dot(o.astype(BF16), wout_ref[...])
    x1 = x_ref[...] + gt1_ref[...] * _rms(y, g_ref[1:2, :])
    h2 = (_rms(x1, g_ref[2:3, :]) * (1.0 + sc2_ref[...]) + sh2_ref[...]).astype(BF16)
    d_ff = wup_ref.shape[1]
    u = jnp.zeros(x1.shape, F32)
    for c0 in range(0, d_ff, ff_chunk):
        a = jnp.square(jnp.maximum(_dot(h2, wup_ref[:, c0:c0 + ff_chunk]), 0.0))
        u = u + _dot(a.astype(BF16), wdown_ref[c0:c0 + ff_chunk, :])
    o_ref[...] = x1 + gt2_ref[...] * _rms(u, g_ref[3:4, :])


def _post_block(x, attn, w_out, norm_g, mods, w_up, w_down, layer, *, mode, tm):
    b, l, d = x.shape
    r = mods[0].shape[1]
    rb = 1 if r == 1 else tm
    mod_map = (lambda bi, i: (bi, 0, 0)) if r == 1 else (lambda bi, i: (bi, i, 0))
    d_ff = w_up.shape[2]
    tok = lambda bi, i: (bi, i, 0)
    in_specs = [pl.BlockSpec((None, tm, d), tok)]
    in_specs += [pl.BlockSpec((None, tm, a.shape[2]), tok) for a in attn]
    in_specs += [pl.BlockSpec((None, d, d), lambda bi, i: (layer // 2, 0, 0)),
                 pl.BlockSpec((None, 4, d), lambda bi, i: (layer, 0, 0))]
    in_specs += [pl.BlockSpec((None, rb, d), mod_map)] * 4
    in_specs += [pl.BlockSpec((None, d, d_ff), lambda bi, i: (layer, 0, 0)),
                 pl.BlockSpec((None, d_ff, d), lambda bi, i: (layer, 0, 0))]
    return pl.pallas_call(
        functools.partial(_post_kernel, mode=mode, n_attn=len(attn), ff_chunk=1024),
        grid=(b, l // tm),
        in_specs=in_specs,
        out_specs=pl.BlockSpec((None, tm, d), tok),
        out_shape=jax.ShapeDtypeStruct((b, l, d), F32),
        compiler_params=_cparams("parallel", "parallel"),
        name="post_block",
    )(x, *attn, w_out, norm_g, *mods, w_up, w_down)


def _decode_cache_kernel(*refs, hq, hkv, mult_new, has_sink, has_gate, has_prev):
    it = iter(refs)
    q_ref, new_ref, c_ref, bias_ref, biasn_ref, mult_ref = [next(it) for _ in range(6)]
    sink_ref = next(it) if has_sink else None
    gate_ref = next(it) if has_gate else None
    if has_prev:
        next(it)
    o_ref, cout_ref = next(it), next(it)
    g = hq // hkv
    r = c_ref.shape[1]
    hd = HEAD_DIM
    s_rows, s_new = [], []
    for kh in range(hkv):
        kt = c_ref[kh * hd:(kh + 1) * hd, :]
        kn = new_ref[kh * hd:(kh + 1) * hd, :]
        for gi in range(g):
            h = kh * g + gi
            qc = q_ref[h * hd:(h + 1) * hd, :] * SCALE
            s_rows.append(jnp.sum(kt * qc, axis=0, keepdims=True))
            s_new.append(jnp.sum(kn * qc, axis=0, keepdims=True))
    mult = mult_ref[...]
    s = jnp.where(mult > 0.0, jnp.concatenate(s_rows, axis=0) + bias_ref[...], NEG)
    sn = jnp.concatenate(s_new, axis=0) + biasn_ref[...]
    m = jnp.maximum(jnp.max(s, axis=1, keepdims=True), sn)
    if has_sink:
        m = jnp.maximum(m, sink_ref[...])
    p = mult * jnp.exp(s - m)
    pn = mult_new * jnp.exp(sn - m)
    l = jnp.sum(p, axis=1, keepdims=True) + pn
    if has_sink:
        l = l + jnp.exp(sink_ref[...] - m)
    inv = 1.0 / l
    if has_gate:
        inv = inv * jax.nn.sigmoid(gate_ref[...])
    for kh in range(hkv):
        vt = c_ref[(hkv + kh) * hd:(hkv + kh + 1) * hd, :]
        vn = new_ref[(hkv + kh) * hd:(hkv + kh + 1) * hd, :]
        for gi in range(g):
            h = kh * g + gi
            o = jnp.sum(vt * p[h:h + 1, :], axis=1, keepdims=True) + vn * pn[h:h + 1, :]
            o_ref[h * hd:(h + 1) * hd, :] = o * inv[h:h + 1, :]
    lane = lax.broadcasted_iota(jnp.int32, (hd, r), 1)
    for c in range(2 * hkv):
        rows = slice(c * hd, (c + 1) * hd)
        shifted = pltpu.roll(c_ref[rows, :], r - 1, axis=1)
        cout_ref[rows, :] = jnp.where(lane == r - 1, new_ref[rows, :], shifted)


def _cache_view(cache):
    lyr, n, r = cache.shape[:3]
    return cache.transpose(0, 1, 3, 4, 5, 2).reshape(lyr, n, -1, r)


def _cache_unview(view, hkv):
    lyr, n, c, r = view.shape
    return view.reshape(lyr, n, 2, hkv, HEAD_DIM, r).transpose(0, 1, 5, 2, 3, 4)


def _decode_cache(q, new, view, layer, tab, patterns, *, hq, hkv, prev, sink=None, gate=None):
    lyr, n, c, wb = view.shape
    dist = wb - jnp.arange(wb)
    mult = sum(((dist <= w) & (dist % d == 0)).astype(F32) for w, d in patterns)[None, :]
    consts = [_bias_of(tab, dist), _bias_of(tab, jnp.zeros((1,), jnp.int32)), mult]
    if sink is not None:
        consts.append(sink.reshape(hq, 1).astype(F32))
    in_specs = [pl.BlockSpec((None, hq * HEAD_DIM, 1), lambda i: (i, 0, 0)),
                pl.BlockSpec((None, c, 1), lambda i: (i, 0, 0)),
                pl.BlockSpec((None, None, c, wb), lambda i: (layer, i, 0, 0))]
    in_specs += [pl.BlockSpec(cst.shape, lambda i: (0, 0)) for cst in consts]
    args = [q.reshape(n, hq * HEAD_DIM, 1), new.reshape(n, c, 1), view] + consts
    if gate is not None:
        in_specs.append(pl.BlockSpec((None, hq, 1), lambda i: (i, 0, 0)))
        args.append(gate.reshape(n, hq, 1))
    in_specs.append(pl.BlockSpec(memory_space=pl.ANY))
    aliases = {len(args): 1}
    args.append(prev)
    o, buf = pl.pallas_call(
        functools.partial(_decode_cache_kernel, hq=hq, hkv=hkv, mult_new=float(len(patterns)),
                          has_sink=sink is not None, has_gate=gate is not None, has_prev=True),
        grid=(n,),
        in_specs=in_specs,
        out_specs=[pl.BlockSpec((None, hq * HEAD_DIM, 1), lambda i: (i, 0, 0)),
                   pl.BlockSpec((None, None, c, wb), lambda i: (layer, i, 0, 0))],
        out_shape=[jax.ShapeDtypeStruct((n, hq * HEAD_DIM, 1), F32),
                   jax.ShapeDtypeStruct((lyr, n, c, wb), F32)],
        input_output_aliases=aliases,
        compiler_params=_cparams("arbitrary"),
        name="decode_cache",
    )(*args)
    return o.reshape(n, hq * HEAD_DIM), buf


def _paged_hab_kernel(pt_ref, *refs, n_pages):
    page_refs = refs[:n_pages]
    perm_ref, w_ref, o_ref, z_ref = refs[n_pages:]
    page = page_refs[0].shape[1]
    cpp = page // CMP_STRIDE
    hw = w_ref.shape[3] // 2
    perm = perm_ref[...]
    for pg in range(n_pages):
        xp = _dot_nt(perm, page_refs[pg][...].astype(BF16))
        for s in range(CMP_STRIDE):
            z_ref[s, pg * cpp:(pg + 1) * cpp, :] = xp[s * cpp:(s + 1) * cpp, :]
    for p in range(2):
        acc = None
        for s2 in range(CMP_STRIDE // 2):
            zs = jnp.concatenate([z_ref[2 * s2, :, p * LANES:(p + 1) * LANES],
                                  z_ref[2 * s2 + 1, :, p * LANES:(p + 1) * LANES]], axis=1)
            part = _dot(zs.astype(BF16), w_ref[s2, p])
            acc = part if acc is None else acc + part
        o_ref[:, p * hw:(p + 1) * hw] = acc[:, :hw]
        o_ref[:, (2 + p) * hw:(3 + p) * hw] = acc[:, hw:]


def _paged_hab(pool_view, page_table, w1p, layer):
    lyr, n_pool, c, page = pool_view.shape
    n, n_pages = page_table.shape
    cpp = page // CMP_STRIDE
    nch = n_pages * cpp
    nout = 2 * w1p.shape[4]
    rows = jnp.arange(page)
    perm = (rows[None, :] == (rows[:, None] % cpp) * CMP_STRIDE + rows[:, None] // cpp).astype(BF16)

    def page_map(k):
        return lambda i, pt: (layer, pt[i, k], 0, 0)

    grid_spec = pltpu.PrefetchScalarGridSpec(
        num_scalar_prefetch=1,
        grid=(n,),
        in_specs=[pl.BlockSpec((None, None, c, page), page_map(k)) for k in range(n_pages)]
        + [pl.BlockSpec(perm.shape, lambda i, pt: (0, 0)),
           pl.BlockSpec((None,) + w1p.shape[1:], lambda i, pt: (layer, 0, 0, 0, 0))],
        out_specs=pl.BlockSpec((None, nch, nout), lambda i, pt: (i, 0, 0)),
        scratch_shapes=[pltpu.VMEM((CMP_STRIDE, nch, c), F32)],
    )
    return pl.pallas_call(
        functools.partial(_paged_hab_kernel, n_pages=n_pages),
        grid_spec=grid_spec,
        out_shape=jax.ShapeDtypeStruct((n, nch, nout), F32),
        compiler_params=_cparams("arbitrary"),
        name="paged_hab",
    )(page_table, *([pool_view] * n_pages), perm, w1p)


def _cmp_sample_kernel(q_ref, ckv_ref, bias_ref, ov_ref, gate_ref, o_ref, idx_ref, *, nch, n_slc, qpos):
    g = NSA_HEADS // NSA_KV_HEADS
    wk = NSA_KV_HEADS * HEAD_DIM
    nseq = q_ref.shape[0]
    c = lax.broadcasted_iota(jnp.int32, (g, nch), 1)
    mask = (c * CMP_STRIDE + CMP_LEN - 1 <= qpos) & (c < nch - 1)
    n_slc_pad = ov_ref.shape[1]
    jb = lax.broadcasted_iota(jnp.int32, (1, n_slc_pad), 1)
    cur = qpos // SLC_BLOCK
    valid = (jb * SLC_BLOCK <= qpos) & (jb < n_slc)
    forced = (jb == 0) | (jb == cur) | (jb == cur - 1)
    k_sel = min(N_SELECT, n_slc)
    imps = []
    for sq in range(nseq):
        ckv = ckv_ref[sq]
        q = q_ref[sq] * SCALE
        gates = jax.nn.sigmoid(gate_ref[sq])
        for kh in range(NSA_KV_HEADS):
            rs = slice(kh * g, (kh + 1) * g)
            k_h = ckv[:, kh * HEAD_DIM:(kh + 1) * HEAD_DIM]
            v_h = ckv[:, wk + kh * HEAD_DIM: wk + (kh + 1) * HEAD_DIM].astype(BF16)
            s = _dot_nt_3pass(q[rs, :], k_h) + bias_ref[rs, :]
            s = jnp.where(mask, s, NEG)
            m = jnp.max(s, axis=-1, keepdims=True)
            p = jnp.where(mask, jnp.exp(s - m), 0.0)
            p = p / jnp.maximum(jnp.sum(p, axis=-1, keepdims=True), 1e-30)
            o_ref[sq, rs, :] = _dot(p.astype(BF16), v_h) * gates[rs, 0:1]
            imp = jnp.sum(_dot_2pass_lhs(p, ov_ref[...]), axis=0, keepdims=True)
            imps.append(jnp.where(valid, imp + jnp.where(forced, FORCE, 0.0), NEG))
    _, picks = _topk_select(jnp.concatenate(imps, axis=0), k_sel)
    lane = lax.broadcasted_iota(jnp.int32, (nseq * NSA_KV_HEADS, LANES), 1)
    rows = jnp.zeros((nseq * NSA_KV_HEADS, LANES), jnp.int32)
    for t, idx in enumerate(picks):
        rows = jnp.where(lane == t, idx.astype(jnp.int32), rows)
    for sq in range(nseq):
        idx_ref[sq] = rows[sq * NSA_KV_HEADS:(sq + 1) * NSA_KV_HEADS, :]


def _cmp_sample(q, ckv, bias, ov, gate, *, n_slc, qpos):
    n = q.shape[0]
    nch = ckv.shape[1]
    nseq = math.gcd(n, CMP_SAMPLE_GROUP)
    return pl.pallas_call(
        functools.partial(_cmp_sample_kernel, nch=nch, n_slc=n_slc, qpos=qpos),
        grid=(n // nseq,),
        in_specs=[pl.BlockSpec((nseq, NSA_HEADS, HEAD_DIM), lambda i: (i, 0, 0)),
                  pl.BlockSpec((nseq, nch, ckv.shape[2]), lambda i: (i, 0, 0)),
                  pl.BlockSpec(bias.shape, lambda i: (0, 0)),
                  pl.BlockSpec(ov.shape, lambda i: (0, 0)),
                  pl.BlockSpec((nseq, NSA_HEADS, 3), lambda i: (i, 0, 0))],
        out_specs=[pl.BlockSpec((nseq, NSA_HEADS, HEAD_DIM), lambda i: (i, 0, 0)),
                   pl.BlockSpec((nseq, NSA_KV_HEADS, LANES), lambda i: (i, 0, 0))],
        out_shape=[jax.ShapeDtypeStruct((n, NSA_HEADS, HEAD_DIM), F32),
                   jax.ShapeDtypeStruct((n, NSA_KV_HEADS, LANES), jnp.int32)],
        compiler_params=_cparams("parallel"),
        name="cmp_sample",
    )(q, ckv, bias, ov, gate)


def _slc_sample_kernel(pg_ref, idx_ref, *refs, k_sel, nb_past):
    n_sel = NSA_KV_HEADS * k_sel
    page_refs = refs[:n_sel]
    q_ref, new_ref, bias_ref, biasn_ref, gate_ref, o_ref = refs[n_sel:]
    i = pl.program_id(0)
    g = NSA_HEADS // NSA_KV_HEADS
    wk = NSA_KV_HEADS * HEAD_DIM
    page = page_refs[0].shape[1]
    bpp = page // SLC_BLOCK
    lane_blk = lax.broadcasted_iota(jnp.int32, (g, page), 1) // SLC_BLOCK
    for kh in range(NSA_KV_HEADS):
        rs = slice(kh * g, (kh + 1) * g)
        ks = slice(kh * HEAD_DIM, (kh + 1) * HEAD_DIM)
        vs = slice(wk + kh * HEAD_DIM, wk + (kh + 1) * HEAD_DIM)
        qg = q_ref[rs, :] * SCALE
        sn = jnp.sum(qg * new_ref[:, ks], axis=-1, keepdims=True) + biasn_ref[rs, :]
        m = sn
        scores = []
        for t in range(k_sel):
            idx = idx_ref[(i * NSA_KV_HEADS + kh) * k_sel + t]
            lp = jnp.minimum(idx, nb_past - 1) // bpp
            kt = page_refs[kh * k_sel + t][ks, :].astype(BF16)
            s = _dot(qg.astype(BF16), kt) + bias_ref[lp, rs, :]
            ok = (lane_blk == idx % bpp) & (idx < nb_past)
            s = jnp.where(ok, s, NEG)
            m = jnp.maximum(m, jnp.max(s, axis=-1, keepdims=True))
            scores.append((s, ok))
        l = jnp.exp(sn - m)
        acc = l * new_ref[:, vs]
        for t in range(k_sel):
            s, ok = scores[t]
            p = jnp.where(ok, jnp.exp(s - m), 0.0)
            l = l + jnp.sum(p, axis=-1, keepdims=True)
            vt = page_refs[kh * k_sel + t][vs, :].astype(BF16)
            acc = acc + _dot_nt(p.astype(BF16), vt)
        o_ref[rs, :] = acc / l * jax.nn.sigmoid(gate_ref[rs, :])


def _slc_sample(q, new, pool_view, layer, pages, idx, bias, bias_new, gate, *, nb_past):
    n = q.shape[0]
    lyr, n_pool, c, page = pool_view.shape
    k_sel = pages.shape[0] // (n * NSA_KV_HEADS)
    n_sel = NSA_KV_HEADS * k_sel

    def page_map(k):
        return lambda i, pg, ids: (layer, pg[i * n_sel + k], 0, 0)

    grid_spec = pltpu.PrefetchScalarGridSpec(
        num_scalar_prefetch=2,
        grid=(n,),
        in_specs=[pl.BlockSpec((None, None, c, page), page_map(k)) for k in range(n_sel)]
        + [pl.BlockSpec((None, NSA_HEADS, HEAD_DIM), lambda i, pg, ids: (i, 0, 0)),
           pl.BlockSpec((None, 1, c), lambda i, pg, ids: (i, 0, 0)),
           pl.BlockSpec(bias.shape, lambda i, pg, ids: (0, 0, 0)),
           pl.BlockSpec(bias_new.shape, lambda i, pg, ids: (0, 0)),
           pl.BlockSpec((None, NSA_HEADS, 1), lambda i, pg, ids: (i, 0, 0))],
        out_specs=pl.BlockSpec((None, NSA_HEADS, HEAD_DIM), lambda i, pg, ids: (i, 0, 0)),
    )
    return pl.pallas_call(
        functools.partial(_slc_sample_kernel, k_sel=k_sel, nb_past=nb_past),
        grid_spec=grid_spec,
        out_shape=jax.ShapeDtypeStruct((n, NSA_HEADS, HEAD_DIM), F32),
        compiler_params=_cparams("arbitrary"),
        name="slc_sample",
    )(pages, idx, *([pool_view] * n_sel), q, new, bias, bias_new, gate)


EVEN_SPLITS = (A_HEADS * HEAD_DIM, 2 * A_KV_HEADS * HEAD_DIM, B_HEADS * HEAD_DIM, 2 * B_HEADS * HEAD_DIM)
NSA_KV_W = 2 * NSA_KV_HEADS * HEAD_DIM
NSA_SPLITS = (NSA_HEADS * HEAD_DIM, NSA_KV_W, NSA_KV_W, NSA_KV_W, LANES)
NSA_SPLITS_T = (NSA_KV_HEADS * HEAD_DIM, NSA_KV_HEADS * HEAD_DIM, 3 * NSA_HEADS)
EVEN_SPLITS_T = (A_KV_HEADS * HEAD_DIM,)


def _mod_parts(mod):
    if mod.ndim == 3:
        return [mod[:, i:i + 1] for i in range(6)]
    return [mod[:, :, i] for i in range(6)]


def _even_layer(xp, xs, mod_p, mod_s, view_swa, view_dil, bufs, li, layer, wts, tabs):
    w_in, w_in_t, w_out, sinks, norm_g, w_up, w_down = wts
    g0 = norm_g[layer, 0:1]
    sh1, sc1, gt1, sh2, sc2, gt2 = _mod_parts(mod_p)
    qa, kva, qb, kvb, vt_a = _norm_linear(xp, g0, sc1, sh1, w_in, li, EVEN_SPLITS, tm=512,
                                          w_t=w_in_t, splits_t=EVEN_SPLITS_T)
    oa = _banded_gqa(qa, kva, vt_a, tabs["swa_t"], hq=A_HEADS, window=WIN_A, sink=sinks[li])
    ob = _dilated_attn(qb, kvb, tabs["dil"], DIL_PAIRS)
    xp_new = _post_block(xp, [oa, ob], w_out, norm_g, (gt1, sc2, sh2, gt2), w_up, w_down, layer,
                         mode="cat", tm=256)
    l = xp.shape[1]
    state_p = (kva[:, l - min(WIN_A, l):], kvb[:, l - min(DIL_MAX_WIN, l):])
    sh1, sc1, gt1, sh2, sc2, gt2 = _mod_parts(mod_s)
    n = xs.shape[1]
    qa, kva, qb, kvb = [a[0] for a in _norm_linear(xs, g0, sc1, sh1, w_in, li, EVEN_SPLITS, tm=n)]
    oa, bufs["swa"] = _decode_cache(qa, kva, view_swa, li, tabs["rel"][:, :A_HEADS], ((WIN_A, 1),),
                                    hq=A_HEADS, hkv=A_KV_HEADS, prev=bufs["swa"], sink=sinks[li])
    ob, bufs["dil"] = _decode_cache(qb, kvb, view_dil, li, tabs["rel"][:, A_HEADS:], DIL_PAIRS,
                                    hq=B_HEADS, hkv=B_HEADS, prev=bufs["dil"])
    o = jnp.concatenate([oa, ob], axis=1)[None]
    xs_new = _post_block(xs, [o], w_out, norm_g, (gt1, sc2, sh2, gt2), w_up, w_down, layer,
                         mode="sum", tm=n)
    return xp_new, xs_new, state_p


def _nsa_layer(xp, xs, mod_p, mod_s, view_cmp, view_slc, view_win, bufs, page_table, li, layer, wts, tabs):
    w_in, w_in_t, w_out, w1big, w2big, w1p, pe_rows, norm_g, w_up, w_down = wts
    g0 = norm_g[layer, 0:1]
    b, l, _ = xp.shape
    pe_h = _linear(pe_rows[li], w1big, li, tm=8, tn=512)
    sh1, sc1, gt1, sh2, sc2, gt2 = _mod_parts(mod_p)
    q, kv_cmp, kv_slc, kv_win, _, vt_slc, vt_win, gate_t = _norm_linear(
        xp, g0, sc1, sh1, w_in, li, NSA_SPLITS, tm=512, w_t=w_in_t, splits_t=NSA_SPLITS_T)
    nch = l // CMP_STRIDE
    n_slc = -(-l // SLC_BLOCK)
    hab = _linear(kv_cmp.reshape(b * nch, CMP_STRIDE * NSA_KV_W), w1big, li, tm=min(512, b * nch), tn=512)
    pad = nch - Q_BLOCK // CMP_STRIDE
    ckv_pad = _compress_finish(hab.reshape(b, nch, -1), pe_h, w2big, li, pad=pad, rows_out=2 * nch)
    o_cmp, imp = _cmp_prompt(q, ckv_pad, tabs["cmp_rel_t"], tabs["ov_pad"], gate_t, nch=nch, n_slc=n_slc)
    n_rows = b * NSA_KV_HEADS * l
    sel = _topk_rows(imp.reshape(n_rows, -1), min(N_SELECT, n_slc), tr=min(1024, n_rows)).reshape(imp.shape)
    o_slc = _slc_prompt(q, kv_slc, vt_slc, sel, tabs["slc_tiles"], gate_t)
    o_win = _banded_gqa(q, kv_win, vt_win, tabs["win_t"], hq=NSA_HEADS, window=WIN_C,
                        gate_t=gate_t, gate_row=2 * NSA_HEADS)
    xp_new = _post_block(xp, [o_cmp, o_slc, o_win], w_out, norm_g, (gt1, sc2, sh2, gt2), w_up, w_down,
                         layer, mode="sum", tm=256)
    state_p = (kv_cmp, kv_slc, kv_win[:, l - min(WIN_C, l):])
    sh1, sc1, gt1, sh2, sc2, gt2 = _mod_parts(mod_s)
    n = xs.shape[1]
    q, kv_cmp_s, kv_slc_s, kv_win_s, gate = [a[0] for a in
                                             _norm_linear(xs, g0, sc1, sh1, w_in, li, NSA_SPLITS, tm=n)]
    n_pages = page_table.shape[1]
    page = view_cmp.shape[3]
    past = n_pages * page
    assert past % SLC_BLOCK == 0 and past % CMP_STRIDE == 0
    nch_s = past // CMP_STRIDE
    nb_past = past // SLC_BLOCK
    n_slc_s = nb_past + 1
    hab_s = _paged_hab(view_cmp, page_table, w1p, li)
    ckv_s = _compress_finish(hab_s, pe_h, w2big, li, pad=0, rows_out=nch_s)
    gate3 = gate[:, :NSA_HEADS * 3].reshape(n, NSA_HEADS, 3)
    o_cmp, idx = _cmp_sample(q.reshape(n, NSA_HEADS, HEAD_DIM), ckv_s, tabs["cmp_s"], tabs["ov_s"], gate3,
                             n_slc=n_slc_s, qpos=past)
    k_sel = min(N_SELECT, n_slc_s)
    idx = idx[:, :, :k_sel]
    bpp = page // SLC_BLOCK
    pidx = jnp.clip(idx, 0, nb_past - 1)
    phys = jnp.take_along_axis(page_table, (pidx // bpp).reshape(n, -1), axis=1)
    o_slc = _slc_sample(q.reshape(n, NSA_HEADS, HEAD_DIM), kv_slc_s[:, None], view_slc, li, phys.reshape(-1),
                        idx.reshape(-1), tabs["slc_s"], tabs["slc_s_new"], gate3[:, :, 1:2], nb_past=nb_past)
    o_win, bufs["win"] = _decode_cache(q, kv_win_s, view_win, li, tabs["rel"], ((WIN_C, 1),), hq=NSA_HEADS,
                                       hkv=NSA_KV_HEADS, prev=bufs["win"], gate=gate3[:, :, 2])
    attn = [o_cmp.reshape(1, n, -1), o_slc.reshape(1, n, -1), o_win[None]]
    xs_new = _post_block(xs, attn, w_out, norm_g, (gt1, sc2, sh2, gt2), w_up, w_down, layer, mode="sum", tm=n)
    return xp_new, xs_new, state_p, (kv_cmp_s, kv_slc_s)


def kernel(x_prompt, x_sample, cache_swa_kv, cache_dil_kv, cache_nsa_cmp, cache_nsa_slc, cache_nsa_win,
           page_table, c_prompt, c_sample, rel_bias, ada_w, ada_b, norm_g, mlp_up, mlp_down,
           even_w_in, even_w_out, attn_sinks, nsa_w_in, nsa_w_out, cmp_w1, cmp_w2, cmp_pe):
    depth, d, _ = ada_w.shape
    b, l, _ = x_prompt.shape
    n = x_sample.shape[0]
    assert x_sample.shape[1] == 1
    assert cache_swa_kv.shape[2] == WIN_A and cache_dil_kv.shape[2] == DIL_MAX_WIN and cache_nsa_win.shape[2] == WIN_C

    ada_wb = ada_w.astype(BF16)
    up_b = mlp_up.astype(BF16)
    down_b = mlp_down.astype(BF16)
    even_in_b = even_w_in.astype(BF16)
    even_out_b = even_w_out.astype(BF16)
    nsa_in_b = jnp.pad(nsa_w_in, ((0, 0), (0, 0), (0, sum(NSA_SPLITS) - nsa_w_in.shape[2]))).astype(BF16)
    nsa_out_b = nsa_w_out.astype(BF16)
    q_w = NSA_HEADS * HEAD_DIM
    v_slc_cols = nsa_w_in[:, :, q_w + NSA_KV_W + NSA_KV_W // 2: q_w + 2 * NSA_KV_W]
    v_win_cols = nsa_w_in[:, :, q_w + 2 * NSA_KV_W + NSA_KV_W // 2: q_w + 3 * NSA_KV_W]
    gate_cols = nsa_w_in[:, :, q_w + 3 * NSA_KV_W:].reshape(-1, d, NSA_HEADS, 3).transpose(0, 1, 3, 2)
    nsa_in_t = jnp.concatenate([v_slc_cols, v_win_cols, gate_cols.reshape(-1, d, 3 * NSA_HEADS)], axis=2)
    nsa_in_t = nsa_in_t.transpose(0, 2, 1).astype(BF16)
    a_w = A_HEADS * HEAD_DIM
    even_in_t = even_w_in[:, :, a_w + A_KV_HEADS * HEAD_DIM: a_w + 2 * A_KV_HEADS * HEAD_DIM]
    even_in_t = even_in_t.transpose(0, 2, 1).astype(BF16)
    w1big, w2big, w1p = _compress_weights(cmp_w1, cmp_w2)
    pe_rows = _pe_rows(cmp_pe)

    c_all = jnp.concatenate([c_prompt, c_sample], axis=0)
    mods = [_linear(c_all, ada_wb, layer, ada_b[:, None, :], pre="silu", tm=b + n, tn=1536)
            for layer in range(depth)]

    nq = l // Q_BLOCK
    nch = l // CMP_STRIDE
    n_slc_pad = _round_up(-(-l // SLC_BLOCK), LANES)
    past = page_table.shape[1] * cache_nsa_cmp.shape[2]
    nch_s = past // CMP_STRIDE
    nb_past = past // SLC_BLOCK
    ov = _overlap_table(nch, n_slc_pad)
    pad = nch - Q_BLOCK // CMP_STRIDE
    page = cache_nsa_slc.shape[2]
    slc_s_dist = past - (jnp.arange(past // page)[:, None] * page + jnp.arange(page)[None, :])
    tabs = {
        "rel": rel_bias,
        "swa_t": _banded_bias_t(rel_bias[:, :A_HEADS], WIN_A),
        "dil": jnp.stack([_banded_bias(rel_bias[:, A_HEADS:], w // dd, dd) for w, dd in DIL_PAIRS]),
        "win_t": _banded_bias_t(rel_bias, WIN_C),
        "cmp_rel_t": _cmp_rel_bias_t(rel_bias, nch),
        "ov_pad": jnp.pad(ov, ((pad, 2 * nch - pad - nch), (0, 0))),
        "slc_tiles": _slc_bias_tiles(rel_bias, nq),
        "cmp_s": _bias_of(rel_bias, past - (jnp.arange(nch_s) * CMP_STRIDE + CMP_LEN - 1)),
        "ov_s": _overlap_table(nch_s, _round_up(nb_past + 1, LANES)),
        "slc_s": _bias_of(rel_bias, slc_s_dist).transpose(1, 0, 2),
        "slc_s_new": _bias_of(rel_bias, jnp.zeros((1,), jnp.int32)),
    }
    views = {"swa": _cache_view(cache_swa_kv), "dil": _cache_view(cache_dil_kv), "win": _cache_view(cache_nsa_win),
             "cmp": _cache_view(cache_nsa_cmp), "slc": _cache_view(cache_nsa_slc)}
    bufs = {k: jnp.zeros(views[k].shape, F32) for k in ("swa", "dil", "win")}

    xp = x_prompt
    xs = x_sample.reshape(1, n, d)
    outs = {k: [] for k in ("swa_p", "dil_p", "cmp_p", "cmp_s", "slc_p", "slc_s", "win_p")}
    for layer in range(depth):
        li = layer // 2
        mod = mods[layer].reshape(b + n, 6, d)
        mod_p, mod_s = mod[:b], mod[b:][None]
        if layer % 2 == 0:
            wts = (even_in_b, even_in_t, even_out_b, attn_sinks, norm_g, up_b, down_b)
            xp, xs, (sa, sb) = _even_layer(xp, xs, mod_p, mod_s, views["swa"], views["dil"], bufs, li,
                                           layer, wts, tabs)
            outs["swa_p"].append(sa.reshape(b, -1, 2, A_KV_HEADS, HEAD_DIM))
            outs["dil_p"].append(sb.reshape(b, -1, 2, B_HEADS, HEAD_DIM))
        else:
            wts = (nsa_in_b, nsa_in_t, nsa_out_b, w1big, w2big, w1p, pe_rows, norm_g, up_b, down_b)
            xp, xs, (pc, ps, pw), (qc, qs_) = _nsa_layer(xp, xs, mod_p, mod_s, views["cmp"], views["slc"],
                                                         views["win"], bufs, page_table, li, layer, wts, tabs)
            kvshape = (2, NSA_KV_HEADS, HEAD_DIM)
            outs["cmp_p"].append(pc.reshape(b, -1, *kvshape))
            outs["slc_p"].append(ps.reshape(b, -1, *kvshape))
            outs["win_p"].append(pw.reshape(b, -1, *kvshape))
            outs["cmp_s"].append(qc.reshape(n, 1, *kvshape))
            outs["slc_s"].append(qs_.reshape(n, 1, *kvshape))
    st = {k: jnp.stack(v) for k, v in outs.items()}
    swa_s = _cache_unview(bufs["swa"], A_KV_HEADS)
    dil_s = _cache_unview(bufs["dil"], B_HEADS)
    win_s = _cache_unview(bufs["win"], NSA_KV_HEADS)
    return (xp, xs.reshape(n, 1, d), st["swa_p"], swa_s, st["dil_p"], dil_s,
            st["cmp_p"], st["cmp_s"], st["slc_p"], st["slc_s"], st["win_p"], win_s)
```
